```python
import jax
import jax.numpy as jnp
from jax import lax
import numpy as np

D_MODEL = 1024
BATCH = 4
SEQ = 8192
DEPTH = 1

N_HEADS = 8
HEAD_DIM = D_MODEL // N_HEADS
ATTN_WIDTH = N_HEADS * HEAD_DIM
ROT_DIM = HEAD_DIM // 4
ROPE_THETA = 500000.0
MOBA_BLOCK = 256
MOBA_TOPK = 3
Q_CHUNK = 32
LRU_WIDTH = D_MODEL
LRU_BLOCKS = 4
LRU_BLOCK_WIDTH = LRU_WIDTH // LRU_BLOCKS
CONV_WIDTH = 4
LRU_C = 8.0
D_FF = 4 * D_MODEL
NORM_EPS = 1e-6
MASK_VALUE = -1e30
SPLITS = (ATTN_WIDTH, 2 * ATTN_WIDTH, 3 * ATTN_WIDTH,
          3 * ATTN_WIDTH + LRU_WIDTH, 3 * ATTN_WIDTH + 2 * LRU_WIDTH,
          3 * ATTN_WIDTH + 2 * LRU_WIDTH + D_MODEL)
IN_COLS = 3 * ATTN_WIDTH + 2 * LRU_WIDTH + 2 * D_MODEL

kernel_name = "hybrid_moba_rglru_sqrelu_layer"


def rms_norm(x, gain):
    xf = x.astype(jnp.float32)
    y = xf * lax.rsqrt(jnp.mean(xf * xf, axis=-1, keepdims=True) + NORM_EPS)
    return (y * gain.astype(jnp.float32)).astype(x.dtype)


def rope_tables(seq_len):
    inv_freq = ROPE_THETA ** (-jnp.arange(0, ROT_DIM, 2, dtype=jnp.float32) / ROT_DIM)
    ang = jnp.arange(seq_len, dtype=jnp.float32)[:, None] * inv_freq[None, :]
    return jnp.cos(ang), jnp.sin(ang)


def partial_rope(x, cos, sin):
    half = ROT_DIM // 2
    x1 = x[..., :half].astype(jnp.float32)
    x2 = x[..., half:ROT_DIM].astype(jnp.float32)
    rot = jnp.concatenate([x1 * cos - x2 * sin, x2 * cos + x1 * sin], axis=-1).astype(x.dtype)
    return jnp.concatenate([rot, x[..., ROT_DIM:]], axis=-1)


def moba_attention(q, k, v):
    b, h, s, dh = q.shape
    n_blocks = -(-s // MOBA_BLOCK)
    pad = n_blocks * MOBA_BLOCK - s
    kp = jnp.pad(k, ((0, 0), (0, 0), (0, pad), (0, 0)))
    vp = jnp.pad(v, ((0, 0), (0, 0), (0, pad), (0, 0)))
    kb = kp.reshape(b, h, n_blocks, MOBA_BLOCK, dh)
    vb = vp.reshape(b, h, n_blocks, MOBA_BLOCK, dh)
    k_mean = jnp.mean(kb.astype(jnp.float32), axis=3)
    top = min(MOBA_TOPK, n_blocks)
    scale = dh ** -0.5
    n_chunks = s // Q_CHUNK
    q_chunks = jnp.moveaxis(q.reshape(b, h, n_chunks, Q_CHUNK, dh), 2, 0)
    block_ids = jnp.arange(n_blocks)
    gather_blocks = jax.vmap(jax.vmap(lambda blocks, idx: blocks[idx]))

    def one_chunk(args):
        c, qc = args
        q_start = c * Q_CHUNK
        q_pos = q_start + jnp.arange(Q_CHUNK)
        q_block = q_start // MOBA_BLOCK
        gate = jnp.einsum("bhqd,bhnd->bhqn", qc.astype(jnp.float32), k_mean)
        gate = jnp.where(block_ids < q_block, gate, -jnp.inf)
        _, sel = lax.top_k(gate, top)
        sel_ok = jnp.arange(top) < q_block
        k_sel = gather_blocks(kb, sel)
        v_sel = gather_blocks(vb, sel)
        s_sel = jnp.einsum("bhqd,bhqjkd->bhqjk", qc, k_sel).astype(jnp.float32) * scale
        s_sel = jnp.where(sel_ok[:, None], s_sel, MASK_VALUE)
        s_sel = s_sel.reshape(b, h, Q_CHUNK, top * MOBA_BLOCK)
        k_own = lax.dynamic_slice_in_dim(kp, q_block * MOBA_BLOCK, MOBA_BLOCK, axis=2)
        v_own = lax.dynamic_slice_in_dim(vp, q_block * MOBA_BLOCK, MOBA_BLOCK, axis=2)
        key_pos = q_block * MOBA_BLOCK + jnp.arange(MOBA_BLOCK)
        s_own = jnp.einsum("bhqd,bhkd->bhqk", qc, k_own).astype(jnp.float32) * scale
        s_own = jnp.where(key_pos[None, :] <= q_pos[:, None], s_own, MASK_VALUE)
        p = jax.nn.softmax(jnp.concatenate([s_sel, s_own], axis=-1), axis=-1).astype(v.dtype)
        p_sel = p[..., :top * MOBA_BLOCK].reshape(b, h, Q_CHUNK, top, MOBA_BLOCK)
        p_own = p[..., top * MOBA_BLOCK:]
        return (jnp.einsum("bhqjk,bhqjkd->bhqd", p_sel, v_sel)
                + jnp.einsum("bhqk,bhkd->bhqd", p_own, v_own))

    out = lax.map(one_chunk, (jnp.arange(n_chunks, dtype=jnp.int32), q_chunks))
    return jnp.moveaxis(out, 0, 2).reshape(b, h, s, dh)


def causal_depthwise_conv(x, w, bias):
    s = x.shape[1]
    xp = jnp.pad(x, ((0, 0), (CONV_WIDTH - 1, 0), (0, 0)))
    y = bias
    for j in range(CONV_WIDTH):
        y = y + w[j] * xp[:, j:j + s]
    return y


def block_diag_linear(x, w, bias):
    b, s, _ = x.shape
    xg = x.reshape(b, s, LRU_BLOCKS, LRU_BLOCK_WIDTH)
    return jnp.einsum("bsgi,gij->bsgj", xg, w).reshape(b, s, LRU_WIDTH) + bias


def _linear_combine(left, right):
    a_l, b_l = left
    a_r, b_r = right
    return a_l * a_r, a_r * b_l + b_r


def rg_lru(x, w_a, b_a, w_i, b_i, lam):
    r = jax.nn.sigmoid(block_diag_linear(x, w_a, b_a).astype(jnp.float32))
    i = jax.nn.sigmoid(block_diag_linear(x, w_i, b_i).astype(jnp.float32))
    log_a = -LRU_C * r * jax.nn.softplus(-lam.astype(jnp.float32))
    a = jnp.exp(log_a)
    mult = jnp.sqrt(-jnp.expm1(2.0 * log_a))
    bx = mult * (i * x.astype(jnp.float32))
    _, h = lax.associative_scan(_linear_combine, (a, bx), axis=1)
    return h.astype(x.dtype)


def hybrid_mixer(u, w_in, conv_w, conv_b, w_rg_a, b_rg_a, w_rg_i, b_rg_i, lru_lambda,
                 w_branch_attn, w_branch_lru, w_out):
    b, s, _ = u.shape
    proj = jnp.einsum("bsd,dc->bsc", u, w_in)
    q, k, v, x_lru, g_lru, gate_attn, gate_lru = jnp.split(proj, list(SPLITS), axis=-1)

    def to_heads(t):
        return t.reshape(b, s, N_HEADS, HEAD_DIM).transpose(0, 2, 1, 3)

    cos, sin = rope_tables(s)
    qh = partial_rope(to_heads(q), cos, sin)
    kh = partial_rope(to_heads(k), cos, sin)
    attn = moba_attention(qh, kh, to_heads(v))
    attn = attn.transpose(0, 2, 1, 3).reshape(b, s, ATTN_WIDTH)
    y_attn = attn @ w_branch_attn

    xc = causal_depthwise_conv(x_lru, conv_w, conv_b)
    hl = rg_lru(xc, w_rg_a, b_rg_a, w_rg_i, b_rg_i, lru_lambda)
    y_lru = (hl * jax.nn.gelu(g_lru)) @ w_branch_lru

    merged = jax.nn.sigmoid(gate_attn) * y_attn + jax.nn.sigmoid(gate_lru) * y_lru
    return merged @ w_out


def setup_inputs(seed: int = 0) -> dict:
    key = jax.random.key(seed)
    ks = jax.random.split(key, 18)
    f32 = jnp.float32

    def nrm(k, shape, scale):
        return jax.random.normal(k, shape, f32) * scale

    def gain(k):
        return 1.0 + 0.05 * jax.random.normal(k, (DEPTH, D_MODEL), f32)

    a_init = jax.random.uniform(ks[10], (DEPTH, LRU_WIDTH), f32, 0.9, 0.999)
    return {
        "x": nrm(ks[0], (BATCH, SEQ, D_MODEL), 1.0),
        "attn_pre_norm": gain(ks[1]),
        "attn_post_norm": gain(ks[2]),
        "w_in": nrm(ks[3], (DEPTH, D_MODEL, IN_COLS), D_MODEL ** -0.5),
        "conv_w": nrm(ks[4], (DEPTH, CONV_WIDTH, LRU_WIDTH), CONV_WIDTH ** -0.5),
        "conv_b": nrm(ks[5], (DEPTH, LRU_WIDTH), 0.01),
        "w_rg_a": nrm(ks[6], (DEPTH, LRU_BLOCKS, LRU_BLOCK_WIDTH, LRU_BLOCK_WIDTH), LRU_BLOCK_WIDTH ** -0.5),
        "b_rg_a": nrm(ks[7], (DEPTH, LRU_WIDTH), 0.01),
        "w_rg_i": nrm(ks[8], (DEPTH, LRU_BLOCKS, LRU_BLOCK_WIDTH, LRU_BLOCK_WIDTH), LRU_BLOCK_WIDTH ** -0.5),
        "b_rg_i": nrm(ks[9], (DEPTH, LRU_WIDTH), 0.01),
        "lru_lambda": jnp.log(a_init) - jnp.log1p(-a_init),
        "w_branch_attn": nrm(ks[11], (DEPTH, ATTN_WIDTH, D_MODEL), ATTN_WIDTH ** -0.5),
        "w_branch_lru": nrm(ks[12], (DEPTH, LRU_WIDTH, D_MODEL), LRU_WIDTH ** -0.5),
        "w_out": nrm(ks[13], (DEPTH, D_MODEL, D_MODEL), D_MODEL ** -0.5),
        "mlp_pre_norm": gain(ks[14]),
        "mlp_post_norm": gain(ks[15]),
        "w_mlp_up": nrm(ks[16], (DEPTH, D_MODEL, D_FF), D_MODEL ** -0.5),
        "w_mlp_down": nrm(ks[17], (DEPTH, D_FF, D_MODEL), D_FF ** -0.5),
    }


def reference(x, attn_pre_norm, attn_post_norm, w_in, conv_w, conv_b, w_rg_a, b_rg_a,
              w_rg_i, b_rg_i, lru_lambda, w_branch_attn, w_branch_lru, w_out,
              mlp_pre_norm, mlp_post_norm, w_mlp_up, w_mlp_down):
    h = x
    for l in range(DEPTH):
        u = rms_norm(h, attn_pre_norm[l])
        mix = hybrid_mixer(u, w_in[l], conv_w[l], conv_b[l], w_rg_a[l], b_rg_a[l],
                           w_rg_i[l], b_rg_i[l], lru_lambda[l],
                           w_branch_attn[l], w_branch_lru[l], w_out[l])
        h = h + rms_norm(mix, attn_post_norm[l])
        u = rms_norm(h, mlp_pre_norm[l])
        m = jnp.square(jax.nn.relu(u @ w_mlp_up[l])) @ w_mlp_down[l]
        h = h + rms_norm(m, mlp_post_norm[l])
    return h
```

```python
import functools

import jax
import jax.numpy as jnp
from jax import lax
from jax.experimental import pallas as pl
from jax.experimental.pallas import tpu as pltpu

N_HEADS = 8
HEAD_DIM = 128
ROT_DIM = 32
ROPE_THETA = 500000.0
MOBA_BLOCK = 256
MOBA_TOPK = 3
LRU_BLOCKS = 4
CONV_WIDTH = 4
LRU_C = 8.0
NORM_EPS = 1e-6
MASK_VALUE = -1e30

LANES = 128
SUBLANES = 8
VMEM_LIMIT = 56 * 1024 * 1024


def _rms(xf, gain):
    return xf * lax.rsqrt(jnp.mean(xf * xf, axis=-1, keepdims=True) + NORM_EPS) * gain


def _in_proj_kernel(x_ref, g_ref, w_ref, o_ref, u_ref):
    @pl.when(pl.program_id(1) == 0)
    def _():
        u_ref[...] = _rms(x_ref[...], g_ref[...]).astype(jnp.bfloat16)

    o_ref[...] = jnp.dot(u_ref[...], w_ref[...],
                         preferred_element_type=jnp.float32).astype(o_ref.dtype)


def _in_proj(x2, gain, w_bf16, tm, tn):
    n, d = x2.shape
    cols = w_bf16.shape[1]
    return pl.pallas_call(
        _in_proj_kernel,
        grid=(n // tm, cols // tn),
        in_specs=[
            pl.BlockSpec((tm, d), lambda i, j: (i, 0)),
            pl.BlockSpec((1, d), lambda i, j: (0, 0)),
            pl.BlockSpec((d, tn), lambda i, j: (0, j)),
        ],
        out_specs=pl.BlockSpec((tm, tn), lambda i, j: (i, j)),
        out_shape=jax.ShapeDtypeStruct((n, cols), jnp.bfloat16),
        scratch_shapes=[pltpu.VMEM((tm, d), jnp.bfloat16)],
        compiler_params=pltpu.CompilerParams(
            dimension_semantics=("arbitrary", "arbitrary"), vmem_limit_bytes=VMEM_LIMIT),
        name="in_proj",
    )(x2, gain, w_bf16)


def _rope(xf, c, s):
    lane = lax.broadcasted_iota(jnp.int32, xf.shape, 1)
    half = ROT_DIM // 2
    partner = jnp.where(lane < half, pltpu.roll(xf, LANES - half, axis=1),
                        pltpu.roll(xf, half, axis=1))
    return xf * c + partner * s


def _moba_kernel(q_ref, k_ref, v_ref, c_ref, s_ref, o_ref, kr_ref, km_ref, *, n_blocks):
    i = pl.program_id(2)
    blk = MOBA_BLOCK

    @pl.when(i == 0)
    def _():
        km_ref[...] = jnp.zeros_like(km_ref)

        def body(c, carry):
            rows = pl.ds(pl.multiple_of(c * blk, blk), blk)
            kc = _rope(k_ref[rows, :].astype(jnp.float32), c_ref[rows, :], s_ref[rows, :])
            kr_ref[rows, :] = kc.astype(jnp.bfloat16)
            km_ref[pl.ds(c, 1), :] = jnp.mean(kc, axis=0, keepdims=True)
            return carry

        lax.fori_loop(0, n_blocks, body, 0)

    rows_i = pl.ds(pl.multiple_of(i * blk, blk), blk)
    qf = _rope(q_ref[...].astype(jnp.float32), c_ref[rows_i, :], s_ref[rows_i, :])
    qs = (qf * (HEAD_DIM ** -0.5)).astype(jnp.bfloat16)

    gate = lax.dot_general(qf, km_ref[...], (((1,), (1,)), ((), ())),
                           precision=lax.Precision.HIGHEST,
                           preferred_element_type=jnp.float32)
    lane = lax.broadcasted_iota(jnp.int32, gate.shape, 1)
    g = jnp.where(lane < i, gate, -jnp.inf)
    sel = jnp.zeros(gate.shape, jnp.float32)
    for r in range(MOBA_TOPK):
        mx = jnp.max(g, axis=-1, keepdims=True)
        idx = jnp.min(jnp.where(g == mx, lane, LANES), axis=-1, keepdims=True)
        hit = lane == idx
        sel = jnp.where(jnp.logical_and(hit, r < i), 1.0, sel)
        g = jnp.where(hit, -jnp.inf, g)

    ki = kr_ref[rows_i, :]
    s = lax.dot_general(qs, ki, (((1,), (1,)), ((), ())), preferred_element_type=jnp.float32)
    rr = lax.broadcasted_iota(jnp.int32, s.shape, 0)
    cc = lax.broadcasted_iota(jnp.int32, s.shape, 1)
    s = jnp.where(cc <= rr, s, MASK_VALUE)
    m0 = jnp.max(s, axis=-1, keepdims=True)
    p = jnp.exp(s - m0)
    l0 = jnp.sum(p, axis=-1, keepdims=True)
    acc0 = jnp.dot(p.astype(jnp.bfloat16), v_ref[rows_i, :], preferred_element_type=jnp.float32)

    def body(j, carry):
        m, l, acc = carry
        rows = pl.ds(pl.multiple_of(j * blk, blk), blk)
        picked = jnp.sum(jnp.where(lane == j, sel, 0.0), axis=-1, keepdims=True) > 0.0
        sj = lax.dot_general(qs, kr_ref[rows, :], (((1,), (1,)), ((), ())),
                             preferred_element_type=jnp.float32)
        sj = jnp.where(picked, sj, MASK_VALUE)
        m_new = jnp.maximum(m, jnp.max(sj, axis=-1, keepdims=True))
        alpha = jnp.exp(m - m_new)
        pj = jnp.exp(sj - m_new)
        l = alpha * l + jnp.sum(pj, axis=-1, keepdims=True)
        acc = alpha * acc + jnp.dot(pj.astype(jnp.bfloat16), v_ref[rows, :],
                                    preferred_element_type=jnp.float32)
        return m_new, l, acc

    _, l, acc = lax.fori_loop(0, i, body, (m0, l0, acc0))
    o_ref[...] = (acc / l).astype(o_ref.dtype)


def _moba_attention(proj, cos_t, sin_t, batch, seq):
    n_blocks = seq // MOBA_BLOCK
    n = batch * seq
    kern = functools.partial(_moba_kernel, n_blocks=n_blocks)
    return pl.pallas_call(
        kern,
        grid=(batch, N_HEADS, n_blocks),
        in_specs=[
            pl.BlockSpec((MOBA_BLOCK, HEAD_DIM), lambda b, h, i: (b * n_blocks + i, h)),
            pl.BlockSpec((seq, HEAD_DIM), lambda b, h, i: (b, N_HEADS + h)),
            pl.BlockSpec((seq, HEAD_DIM), lambda b, h, i: (b, 2 * N_HEADS + h)),
            pl.BlockSpec((seq, LANES), lambda b, h, i: (0, 0)),
            pl.BlockSpec((seq, LANES), lambda b, h, i: (0, 0)),
        ],
        out_specs=pl.BlockSpec((MOBA_BLOCK, HEAD_DIM), lambda b, h, i: (b * n_blocks + i, h)),
        out_shape=jax.ShapeDtypeStruct((n, N_HEADS * HEAD_DIM), jnp.bfloat16),
        scratch_shapes=[pltpu.VMEM((seq, HEAD_DIM), jnp.bfloat16),
                        pltpu.VMEM((LANES, HEAD_DIM), jnp.float32)],
        compiler_params=pltpu.CompilerParams(
            dimension_semantics=("arbitrary", "arbitrary", "arbitrary"),
            vmem_limit_bytes=VMEM_LIMIT),
        name="moba_attn",
    )(proj, proj, proj, cos_t, sin_t)


def _gelu_tanh(x):
    return 0.5 * x * (1.0 + jnp.tanh(0.7978845608028654 * (x + 0.044715 * (x * x * x))))


def _sigmoid(x):
    return 1.0 / (1.0 + jnp.exp(-x))


def _rglru_kernel(x_ref, g_ref, cw_ref, cb_ref, wa_ref, ba_ref, wi_ref, bi_ref, lam_ref,
                  o_ref, xe_ref, h_ref, *, t):
    c = pl.program_id(1)
    pad = SUBLANES

    @pl.when(c == 0)
    def _():
        xe_ref[0:pad, :] = jnp.zeros((pad, xe_ref.shape[1]), jnp.float32)
        h_ref[...] = jnp.zeros_like(h_ref)

    @pl.when(c > 0)
    def _():
        xe_ref[0:pad, :] = xe_ref[t:t + pad, :]

    xe_ref[pad:pad + t, :] = x_ref[...].astype(jnp.float32)

    xc = cb_ref[...]
    for j in range(CONV_WIDTH):
        lo = pad - (CONV_WIDTH - 1) + j
        xc = xc + cw_ref[j:j + 1, :] * xe_ref[lo:lo + t, :]

    width = xc.shape[1]
    bw = width // LRU_BLOCKS
    xb = xc.astype(jnp.bfloat16)
    ra = jnp.concatenate(
        [jnp.dot(xb[:, g * bw:(g + 1) * bw], wa_ref[g], preferred_element_type=jnp.float32)
         for g in range(LRU_BLOCKS)], axis=1) + ba_ref[...]
    ri = jnp.concatenate(
        [jnp.dot(xb[:, g * bw:(g + 1) * bw], wi_ref[g], preferred_element_type=jnp.float32)
         for g in range(LRU_BLOCKS)], axis=1) + bi_ref[...]
    r = _sigmoid(ra)
    gi = _sigmoid(ri)
    z = -lam_ref[...]
    softplus = jnp.maximum(z, 0.0) + jnp.log(1.0 + jnp.exp(-jnp.abs(z)))
    a = jnp.exp((-LRU_C) * r * softplus)
    b = jnp.sqrt(1.0 - a * a) * (gi * xc)

    row = lax.broadcasted_iota(jnp.int32, a.shape, 0)
    d = 1
    while d < t:
        a_sh = pltpu.roll(a, d, axis=0)
        b_sh = pltpu.roll(b, d, axis=0)
        keep = row >= d
        b = jnp.where(keep, a * b_sh + b, b)
        a = jnp.where(keep, a * a_sh, a)
        d *= 2
    h = b + a * h_ref[...]
    h_ref[...] = h[t - 1:t, :]
    o_ref[...] = (h * _gelu_tanh(g_ref[...].astype(jnp.float32))).astype(o_ref.dtype)


def _rglru(proj, conv_w, conv_b, wa, ba, wi, bi, lam, batch, seq, t):
    n = batch * seq
    width = conv_w.shape[1]
    nc = seq // t
    xcol = (3 * N_HEADS * HEAD_DIM) // width
    kern = functools.partial(_rglru_kernel, t=t)
    vec = lambda: pl.BlockSpec((1, width), lambda b, c: (0, 0))
    wspec = lambda: pl.BlockSpec((LRU_BLOCKS, width // LRU_BLOCKS, width // LRU_BLOCKS),
                                 lambda b, c: (0, 0, 0))
    return pl.pallas_call(
        kern,
        grid=(batch, nc),
        in_specs=[
            pl.BlockSpec((t, width), lambda b, c: (b * nc + c, xcol)),
            pl.BlockSpec((t, width), lambda b, c: (b * nc + c, xcol + 1)),
            pl.BlockSpec((CONV_WIDTH, width), lambda b, c: (0, 0)),
            vec(), wspec(), vec(), wspec(), vec(), vec(),
        ],
        out_specs=pl.BlockSpec((t, width), lambda b, c: (b * nc + c, 0)),
        out_shape=jax.ShapeDtypeStruct((n, width), jnp.bfloat16),
        scratch_shapes=[pltpu.VMEM((t + 2 * SUBLANES, width), jnp.float32),
                        pltpu.VMEM((1, width), jnp.float32)],
        compiler_params=pltpu.CompilerParams(
            dimension_semantics=("arbitrary", "arbitrary"), vmem_limit_bytes=VMEM_LIMIT),
        name="rglru",
    )(proj, proj, conv_w, conv_b, wa, ba, wi, bi, lam)


def _merge_kernel(x_ref, at_ref, lr_ref, ga_ref, gl_ref, wba_ref, wbl_ref, wo_ref, gain_ref,
                  o_ref):
    ya = jnp.dot(at_ref[...], wba_ref[...], preferred_element_type=jnp.float32)
    yl = jnp.dot(lr_ref[...], wbl_ref[...], preferred_element_type=jnp.float32)
    merged = (_sigmoid(ga_ref[...].astype(jnp.float32)) * ya
              + _sigmoid(gl_ref[...].astype(jnp.float32)) * yl)
    mix = jnp.dot(merged.astype(jnp.bfloat16), wo_ref[...], preferred_element_type=jnp.float32)
    o_ref[...] = x_ref[...] + _rms(mix, gain_ref[...])


def _merge(x2, attn, lru, proj, wba, wbl, wo, gain, tm):
    n, d = x2.shape
    gcol = (3 * N_HEADS * HEAD_DIM + 2 * d) // d
    row = lambda col: pl.BlockSpec((tm, d), lambda i: (i, col))
    wsp = lambda: pl.BlockSpec((d, d), lambda i: (0, 0))
    return pl.pallas_call(
        _merge_kernel,
        grid=(n // tm,),
        in_specs=[row(0), row(0), row(0), row(gcol), row(gcol + 1), wsp(), wsp(), wsp(),
                  pl.BlockSpec((1, d), lambda i: (0, 0))],
        out_specs=row(0),
        out_shape=jax.ShapeDtypeStruct((n, d), jnp.float32),
        compiler_params=pltpu.CompilerParams(
            dimension_semantics=("arbitrary",), vmem_limit_bytes=VMEM_LIMIT),
        name="merge",
    )(x2, attn, lru, proj, proj, wba, wbl, wo, gain)


def _mlp_kernel(h_ref, gpre_ref, wu_ref, wd_ref, gpost_ref, o_ref, *, fc):
    h = h_ref[...]
    u = _rms(h, gpre_ref[...]).astype(jnp.bfloat16)
    d_ff = wu_ref.shape[1]
    acc = jnp.zeros(h.shape, jnp.float32)
    for f in range(d_ff // fc):
        a = jnp.dot(u, wu_ref[:, f * fc:(f + 1) * fc], preferred_element_type=jnp.float32)
        a = jnp.square(jnp.maximum(a, 0.0)).astype(jnp.bfloat16)
        acc = acc + jnp.dot(a, wd_ref[f * fc:(f + 1) * fc, :], preferred_element_type=jnp.float32)
    o_ref[...] = h + _rms(acc, gpost_ref[...])


def _mlp(h1, gpre, wu, wd, gpost, tm, fc):
    n, d = h1.shape
    d_ff = wu.shape[1]
    kern = functools.partial(_mlp_kernel, fc=fc)
    vec = lambda: pl.BlockSpec((1, d), lambda i: (0, 0))
    return pl.pallas_call(
        kern,
        grid=(n // tm,),
        in_specs=[pl.BlockSpec((tm, d), lambda i: (i, 0)), vec(),
                  pl.BlockSpec((d, d_ff), lambda i: (0, 0)),
                  pl.BlockSpec((d_ff, d), lambda i: (0, 0)), vec()],
        out_specs=pl.BlockSpec((tm, d), lambda i: (i, 0)),
        out_shape=jax.ShapeDtypeStruct((n, d), jnp.float32),
        compiler_params=pltpu.CompilerParams(
            dimension_semantics=("arbitrary",), vmem_limit_bytes=VMEM_LIMIT),
        name="mlp",
    )(h1, gpre, wu, wd, gpost)


def _rope_tables(seq):
    half = ROT_DIM // 2
    inv_freq = ROPE_THETA ** (-jnp.arange(0, ROT_DIM, 2, dtype=jnp.float32) / ROT_DIM)
    ang = jnp.arange(seq, dtype=jnp.float32)[:, None] * inv_freq[None, :]
    cos, sin = jnp.cos(ang), jnp.sin(ang)
    ones = jnp.ones((seq, LANES - ROT_DIM), jnp.float32)
    zeros = jnp.zeros((seq, LANES - ROT_DIM), jnp.float32)
    return (jnp.concatenate([cos, cos, ones], axis=1),
            jnp.concatenate([-sin, sin, zeros], axis=1))


def _layer(h, p, l):
    batch, seq, d = h.shape
    n = batch * seq
    bf = jnp.bfloat16
    row = lambda v: v[l].reshape(1, -1)
    x2 = h.reshape(n, d)
    proj = _in_proj(x2, row(p["attn_pre_norm"]), p["w_in"][l].astype(bf), tm=min(1024, n), tn=1024)
    cos_t, sin_t = _rope_tables(seq)
    attn = _moba_attention(proj, cos_t, sin_t, batch, seq)
    lru = _rglru(proj, p["conv_w"][l], row(p["conv_b"]), p["w_rg_a"][l].astype(bf), row(p["b_rg_a"]),
                 p["w_rg_i"][l].astype(bf), row(p["b_rg_i"]), row(p["lru_lambda"]),
                 batch, seq, t=min(256, seq))
    h1 = _merge(x2, attn, lru, proj, p["w_branch_attn"][l].astype(bf),
                p["w_branch_lru"][l].astype(bf), p["w_out"][l].astype(bf),
                row(p["attn_post_norm"]), tm=min(512, n))
    out = _mlp(h1, row(p["mlp_pre_norm"]), p["w_mlp_up"][l].astype(bf),
               p["w_mlp_down"][l].astype(bf), row(p["mlp_post_norm"]), tm=min(512, n), fc=1024)
    return out.reshape(batch, seq, d)


def kernel(x, attn_pre_norm, attn_post_norm, w_in, conv_w, conv_b, w_rg_a, b_rg_a, w_rg_i,
           b_rg_i, lru_lambda, w_branch_attn, w_branch_lru, w_out, mlp_pre_norm, mlp_post_norm,
           w_mlp_up, w_mlp_down):
    p = dict(attn_pre_norm=attn_pre_norm, attn_post_norm=attn_post_norm, w_in=w_in, conv_w=conv_w,
             conv_b=conv_b, w_rg_a=w_rg_a, b_rg_a=b_rg_a, w_rg_i=w_rg_i, b_rg_i=b_rg_i,
             lru_lambda=lru_lambda, w_branch_attn=w_branch_attn, w_branch_lru=w_branch_lru,
             w_out=w_out, mlp_pre_norm=mlp_pre_norm, mlp_post_norm=mlp_post_norm,
             w_mlp_up=w_mlp_up, w_mlp_down=w_mlp_down)
    h = x
    for l in range(w_in.shape[0]):
        h = _layer(h, p, l)
    return h
```

```python
import functools

import jax
import jax.numpy as jnp
from jax import lax
from jax.experimental import pallas as pl
from jax.experimental.pallas import tpu as pltpu

N_HEADS = 8
HEAD_DIM = 128
ROT_DIM = 32
ROPE_THETA = 500000.0
MOBA_BLOCK = 256
MOBA_TOPK = 3
LRU_BLOCKS = 4
CONV_WIDTH = 4
LRU_C = 8.0
NORM_EPS = 1e-6
MASK_VALUE = -1e30

LANES = 128
SUBLANES = 8
VMEM_LIMIT = 56 * 1024 * 1024


def _rms(xf, gain):
    return xf * lax.rsqrt(jnp.mean(xf * xf, axis=-1, keepdims=True) + NORM_EPS) * gain


def _in_proj_kernel(x_ref, g_ref, w_ref, o_ref, u_ref):
    @pl.when(pl.program_id(1) == 0)
    def _():
        u_ref[...] = _rms(x_ref[...], g_ref[...]).astype(jnp.bfloat16)

    o_ref[...] = jnp.dot(u_ref[...], w_ref[...],
                         preferred_element_type=jnp.float32).astype(o_ref.dtype)


def _in_proj(x2, gain, w_bf16, tm, tn):
    n, d = x2.shape
    cols = w_bf16.shape[1]
    return pl.pallas_call(
        _in_proj_kernel,
        grid=(n // tm, cols // tn),
        in_specs=[
            pl.BlockSpec((tm, d), lambda i, j: (i, 0)),
            pl.BlockSpec((1, d), lambda i, j: (0, 0)),
            pl.BlockSpec((d, tn), lambda i, j: (0, j)),
        ],
        out_specs=pl.BlockSpec((tm, tn), lambda i, j: (i, j)),
        out_shape=jax.ShapeDtypeStruct((n, cols), jnp.bfloat16),
        scratch_shapes=[pltpu.VMEM((tm, d), jnp.bfloat16)],
        compiler_params=pltpu.CompilerParams(
            dimension_semantics=("arbitrary", "arbitrary"), vmem_limit_bytes=VMEM_LIMIT),
        name="in_proj",
    )(x2, gain, w_bf16)


def _rope(xf, c, s):
    lane = lax.broadcasted_iota(jnp.int32, xf.shape, 1)
    half = ROT_DIM // 2
    partner = jnp.where(lane < half, pltpu.roll(xf, LANES - half, axis=1),
                        pltpu.roll(xf, half, axis=1))
    return xf * c + partner * s


def _split_bf16(xf):
    hi = xf.astype(jnp.bfloat16)
    lo = (xf - hi.astype(jnp.float32)).astype(jnp.bfloat16)
    return hi, lo


def _nt_dot(a, b):
    return lax.dot_general(a, b, (((1,), (1,)), ((), ())), preferred_element_type=jnp.float32)


def _moba_kernel(q_ref, k_ref, v_ref, c_ref, s_ref, o_ref,
                 kr_ref, vt_ref, kmh_ref, kml_ref, sel_ref, sc_ref, *, n_blocks, grp):
    i = pl.program_id(2)
    blk = MOBA_BLOCK

    @pl.when(i == 0)
    def _():
        def body(c, carry):
            rows = pl.ds(pl.multiple_of(c * blk, blk), blk)
            kc = _rope(k_ref[rows, :].astype(jnp.float32), c_ref[rows, :], s_ref[rows, :])
            kr_ref[rows, :] = kc.astype(jnp.bfloat16)
            hi, lo = _split_bf16(jnp.mean(kc, axis=0, keepdims=True))
            kmh_ref[pl.ds(c, 1), :] = hi.astype(jnp.float32)
            kml_ref[pl.ds(c, 1), :] = lo.astype(jnp.float32)
            vt_ref[c] = v_ref[rows, :].astype(jnp.float32).T.astype(jnp.bfloat16)
            return carry

        lax.fori_loop(0, n_blocks, body, 0)

    rows_i = pl.ds(pl.multiple_of(i * blk, blk), blk)
    qf = _rope(q_ref[...].astype(jnp.float32), c_ref[rows_i, :], s_ref[rows_i, :])
    qs = (qf * (HEAD_DIM ** -0.5 * 1.4426950408889634)).astype(jnp.bfloat16)

    q_hi, q_lo = _split_bf16(qf)
    km_hi = kmh_ref[...].astype(jnp.bfloat16)
    km_lo = kml_ref[...].astype(jnp.bfloat16)
    gate = _nt_dot(km_hi, q_hi) + (_nt_dot(km_hi, q_lo) + _nt_dot(km_lo, q_hi))
    blk_id = lax.broadcasted_iota(jnp.int32, gate.shape, 0)
    g = jnp.where(blk_id < i, gate, -jnp.inf)
    sel = jnp.zeros(gate.shape, jnp.float32)
    for r in range(MOBA_TOPK):
        mx = jnp.max(g, axis=0, keepdims=True)
        idx = jnp.min(jnp.where(g == mx, blk_id, n_blocks), axis=0, keepdims=True)
        hit = blk_id == idx
        sel = jnp.where(jnp.logical_and(hit, r < i), 1.0, sel)
        g = jnp.where(hit, -jnp.inf, g)
    for j in range(n_blocks):
        sel_ref[j] = sel[j:j + 1, :]

    s = _nt_dot(kr_ref[rows_i, :], qs)
    kk = lax.broadcasted_iota(jnp.int32, s.shape, 0)
    qq = lax.broadcasted_iota(jnp.int32, s.shape, 1)
    s = jnp.where(kk <= qq, s, MASK_VALUE)
    sc_ref[0] = s
    m0 = jnp.max(s, axis=0, keepdims=True)
    n_groups = (i + grp - 1) // grp

    def pass1(g, m):
        j0 = pl.multiple_of(g * grp, grp)
        rows = pl.ds(pl.multiple_of(j0 * blk, grp * blk), grp * blk)
        sg = _nt_dot(kr_ref[rows, :], qs).reshape(grp, blk, blk)
        picked = sel_ref[pl.ds(j0, grp)] > 0.0
        sg = jnp.where(picked, sg, MASK_VALUE)
        sc_ref[pl.ds(1 + j0, grp)] = sg
        return jnp.maximum(m, jnp.max(jnp.max(sg, axis=0), axis=0, keepdims=True))

    m = lax.fori_loop(0, n_groups, pass1, m0)

    p = jnp.exp2(sc_ref[0] - m)
    l0 = jnp.sum(p, axis=0, keepdims=True)
    acc0 = jnp.dot(vt_ref[i], p.astype(jnp.bfloat16), preferred_element_type=jnp.float32)

    def pass2(g, carry):
        l, acc = carry
        j0 = pl.multiple_of(g * grp, grp)
        for c in range(grp):
            pc = jnp.exp2(sc_ref[1 + j0 + c] - m)
            l = l + jnp.sum(pc, axis=0, keepdims=True)
            acc = acc + jnp.dot(vt_ref[j0 + c], pc.astype(jnp.bfloat16),
                                preferred_element_type=jnp.float32)
        return l, acc

    l, acc = lax.fori_loop(0, n_groups, pass2, (l0, acc0))
    o_ref[...] = (acc / l).T.astype(o_ref.dtype)


def _moba_attention(proj, cos_t, sin_t, batch, seq):
    n_blocks = seq // MOBA_BLOCK
    n = batch * seq
    grp = 4 if n_blocks % 4 == 0 else (2 if n_blocks % 2 == 0 else 1)
    kern = functools.partial(_moba_kernel, n_blocks=n_blocks, grp=grp)
    return pl.pallas_call(
        kern,
        grid=(batch, N_HEADS, n_blocks),
        in_specs=[
            pl.BlockSpec((MOBA_BLOCK, HEAD_DIM), lambda b, h, i: (b * n_blocks + i, h)),
            pl.BlockSpec((seq, HEAD_DIM), lambda b, h, i: (b, N_HEADS + h)),
            pl.BlockSpec((seq, HEAD_DIM), lambda b, h, i: (b, 2 * N_HEADS + h)),
            pl.BlockSpec((seq, LANES), lambda b, h, i: (0, 0)),
            pl.BlockSpec((seq, LANES), lambda b, h, i: (0, 0)),
        ],
        out_specs=pl.BlockSpec((MOBA_BLOCK, HEAD_DIM), lambda b, h, i: (b * n_blocks + i, h)),
        out_shape=jax.ShapeDtypeStruct((n, N_HEADS * HEAD_DIM), jnp.bfloat16),
        scratch_shapes=[pltpu.VMEM((seq, HEAD_DIM), jnp.bfloat16),
                        pltpu.VMEM((n_blocks, HEAD_DIM, MOBA_BLOCK), jnp.bfloat16),
                        pltpu.VMEM((n_blocks, HEAD_DIM), jnp.float32),
                        pltpu.VMEM((n_blocks, HEAD_DIM), jnp.float32),
                        pltpu.VMEM((n_blocks, 1, MOBA_BLOCK), jnp.float32),
                        pltpu.VMEM((1 + n_blocks, MOBA_BLOCK, MOBA_BLOCK), jnp.float32)],
        compiler_params=pltpu.CompilerParams(
            dimension_semantics=("arbitrary", "arbitrary", "arbitrary"),
            vmem_limit_bytes=VMEM_LIMIT),
        name="moba_attn",
    )(proj, proj, proj, cos_t, sin_t)


def _gelu_tanh(x):
    return 0.5 * x * (1.0 + jnp.tanh(0.7978845608028654 * (x + 0.044715 * (x * x * x))))


def _sigmoid(x):
    return 1.0 / (1.0 + jnp.exp(-x))


def _rglru_kernel(x_ref, g_ref, cw_ref, cb_ref, wa_ref, ba_ref, wi_ref, bi_ref, lam_ref,
                  o_ref, xe_ref, h_ref, *, t):
    c = pl.program_id(1)
    pad = SUBLANES

    @pl.when(c == 0)
    def _():
        xe_ref[0:pad, :] = jnp.zeros((pad, xe_ref.shape[1]), jnp.float32)
        h_ref[...] = jnp.zeros_like(h_ref)

    @pl.when(c > 0)
    def _():
        xe_ref[0:pad, :] = xe_ref[t:t + pad, :]

    xe_ref[pad:pad + t, :] = x_ref[...].astype(jnp.float32)

    xc = cb_ref[...]
    for j in range(CONV_WIDTH):
        lo = pad - (CONV_WIDTH - 1) + j
        xc = xc + cw_ref[j:j + 1, :] * xe_ref[lo:lo + t, :]

    width = xc.shape[1]
    bw = width // LRU_BLOCKS
    xb = xc.astype(jnp.bfloat16)
    ra = jnp.concatenate(
        [jnp.dot(xb[:, g * bw:(g + 1) * bw], wa_ref[g], preferred_element_type=jnp.float32)
         for g in range(LRU_BLOCKS)], axis=1) + ba_ref[...]
    ri = jnp.concatenate(
        [jnp.dot(xb[:, g * bw:(g + 1) * bw], wi_ref[g], preferred_element_type=jnp.float32)
         for g in range(LRU_BLOCKS)], axis=1) + bi_ref[...]
    r = _sigmoid(ra)
    gi = _sigmoid(ri)
    z = -lam_ref[...]
    softplus = jnp.maximum(z, 0.0) + jnp.log(1.0 + jnp.exp(-jnp.abs(z)))
    a = jnp.exp((-LRU_C) * r * softplus)
    b = jnp.sqrt(1.0 - a * a) * (gi * xc)

    row = lax.broadcasted_iota(jnp.int32, a.shape, 0)
    d = 1
    while d < t:
        a_sh = pltpu.roll(a, d, axis=0)
        b_sh = pltpu.roll(b, d, axis=0)
        keep = row >= d
        b = jnp.where(keep, a * b_sh + b, b)
        a = jnp.where(keep, a * a_sh, a)
        d *= 2
    h = b + a * h_ref[...]
    h_ref[...] = h[t - 1:t, :]
    o_ref[...] = (h * _gelu_tanh(g_ref[...].astype(jnp.float32))).astype(o_ref.dtype)


def _rglru(proj, conv_w, conv_b, wa, ba, wi, bi, lam, batch, seq, t):
    n = batch * seq
    width = conv_w.shape[1]
    nc = seq // t
    xcol = (3 * N_HEADS * HEAD_DIM) // width
    kern = functools.partial(_rglru_kernel, t=t)
    vec = lambda: pl.BlockSpec((1, width), lambda b, c: (0, 0))
    wspec = lambda: pl.BlockSpec((LRU_BLOCKS, width // LRU_BLOCKS, width // LRU_BLOCKS),
                                 lambda b, c: (0, 0, 0))
    return pl.pallas_call(
        kern,
        grid=(batch, nc),
        in_specs=[
            pl.BlockSpec((t, width), lambda b, c: (b * nc + c, xcol)),
            pl.BlockSpec((t, width), lambda b, c: (b * nc + c, xcol + 1)),
            pl.BlockSpec((CONV_WIDTH, width), lambda b, c: (0, 0)),
            vec(), wspec(), vec(), wspec(), vec(), vec(),
        ],
        out_specs=pl.BlockSpec((t, width), lambda b, c: (b * nc + c, 0)),
        out_shape=jax.ShapeDtypeStruct((n, width), jnp.bfloat16),
        scratch_shapes=[pltpu.VMEM((t + 2 * SUBLANES, width), jnp.float32),
                        pltpu.VMEM((1, width), jnp.float32)],
        compiler_params=pltpu.CompilerParams(
            dimension_semantics=("arbitrary", "arbitrary"), vmem_limit_bytes=VMEM_LIMIT),
        name="rglru",
    )(proj, proj, conv_w, conv_b, wa, ba, wi, bi, lam)


def _merge_kernel(x_ref, at_ref, lr_ref, ga_ref, gl_ref, wba_ref, wbl_ref, wo_ref, gain_ref,
                  o_ref):
    ya = jnp.dot(at_ref[...], wba_ref[...], preferred_element_type=jnp.float32)
    yl = jnp.dot(lr_ref[...], wbl_ref[...], preferred_element_type=jnp.float32)
    merged = (_sigmoid(ga_ref[...].astype(jnp.float32)) * ya
              + _sigmoid(gl_ref[...].astype(jnp.float32)) * yl)
    mix = jnp.dot(merged.astype(jnp.bfloat16), wo_ref[...], preferred_element_type=jnp.float32)
    o_ref[...] = x_ref[...] + _rms(mix, gain_ref[...])


def _merge(x2, attn, lru, proj, wba, wbl, wo, gain, tm):
    n, d = x2.shape
    gcol = (3 * N_HEADS * HEAD_DIM + 2 * d) // d
    row = lambda col: pl.BlockSpec((tm, d), lambda i: (i, col))
    wsp = lambda: pl.BlockSpec((d, d), lambda i: (0, 0))
    return pl.pallas_call(
        _merge_kernel,
        grid=(n // tm,),
        in_specs=[row(0), row(0), row(0), row(gcol), row(gcol + 1), wsp(), wsp(), wsp(),
                  pl.BlockSpec((1, d), lambda i: (0, 0))],
        out_specs=row(0),
        out_shape=jax.ShapeDtypeStruct((n, d), jnp.float32),
        compiler_params=pltpu.CompilerParams(
            dimension_semantics=("arbitrary",), vmem_limit_bytes=VMEM_LIMIT),
        name="merge",
    )(x2, attn, lru, proj, proj, wba, wbl, wo, gain)


def _mlp_kernel(h_ref, gpre_ref, wu_ref, wd_ref, gpost_ref, o_ref, *, fc):
    h = h_ref[...]
    u = _rms(h, gpre_ref[...]).astype(jnp.bfloat16)
    d_ff = wu_ref.shape[1]
    acc = jnp.zeros(h.shape, jnp.float32)
    for f in range(d_ff // fc):
        a = jnp.dot(u, wu_ref[:, f * fc:(f + 1) * fc], preferred_element_type=jnp.float32)
        a = jnp.square(jnp.maximum(a, 0.0)).astype(jnp.bfloat16)
        acc = acc + jnp.dot(a, wd_ref[f * fc:(f + 1) * fc, :], preferred_element_type=jnp.float32)
    o_ref[...] = h + _rms(acc, gpost_ref[...])


def _mlp(h1, gpre, wu, wd, gpost, tm, fc):
    n, d = h1.shape
    d_ff = wu.shape[1]
    kern = functools.partial(_mlp_kernel, fc=fc)
    vec = lambda: pl.BlockSpec((1, d), lambda i: (0, 0))
    return pl.pallas_call(
        kern,
        grid=(n // tm,),
        in_specs=[pl.BlockSpec((tm, d), lambda i: (i, 0)), vec(),
                  pl.BlockSpec((d, d_ff), lambda i: (0, 0)),
                  pl.BlockSpec((d_ff, d), lambda i: (0, 0)), vec()],
        out_specs=pl.BlockSpec((tm, d), lambda i: (i, 0)),
        out_shape=jax.ShapeDtypeStruct((n, d), jnp.float32),
        compiler_params=pltpu.CompilerParams(
            dimension_semantics=("arbitrary",), vmem_limit_bytes=VMEM_LIMIT),
        name="mlp",
    )(h1, gpre, wu, wd, gpost)


def _rope_tables(seq):
    half = ROT_DIM // 2
    inv_freq = ROPE_THETA ** (-jnp.arange(0, ROT_DIM, 2, dtype=jnp.float32) / ROT_DIM)
    ang = jnp.arange(seq, dtype=jnp.float32)[:, None] * inv_freq[None, :]
    cos, sin = jnp.cos(ang), jnp.sin(ang)
    ones = jnp.ones((seq, LANES - ROT_DIM), jnp.float32)
    zeros = jnp.zeros((seq, LANES - ROT_DIM), jnp.float32)
    return (jnp.concatenate([cos, cos, ones], axis=1),
            jnp.concatenate([-sin, sin, zeros], axis=1))


def _layer(h, p, l):
    batch, seq, d = h.shape
    n = batch * seq
    bf = jnp.bfloat16
    row = lambda v: v[l].reshape(1, -1)
    x2 = h.reshape(n, d)
    proj = _in_proj(x2, row(p["attn_pre_norm"]), p["w_in"][l].astype(bf), tm=min(1024, n), tn=1024)
    cos_t, sin_t = _rope_tables(seq)
    attn = _moba_attention(proj, cos_t, sin_t, batch, seq)
    lru = _rglru(proj, p["conv_w"][l], row(p["conv_b"]), p["w_rg_a"][l].astype(bf), row(p["b_rg_a"]),
                 p["w_rg_i"][l].astype(bf), row(p["b_rg_i"]), row(p["lru_lambda"]),
                 batch, seq, t=min(256, seq))
    h1 = _merge(x2, attn, lru, proj, p["w_branch_attn"][l].astype(bf),
                p["w_branch_lru"][l].astype(bf), p["w_out"][l].astype(bf),
                row(p["attn_post_norm"]), tm=min(512, n))
    out = _mlp(h1, row(p["mlp_pre_norm"]), p["w_mlp_up"][l].astype(bf),
               p["w_mlp_down"][l].astype(bf), row(p["mlp_post_norm"]), tm=min(512, n), fc=1024)
    return out.reshape(batch, seq, d)


def kernel(x, attn_pre_norm, attn_post_norm, w_in, conv_w, conv_b, w_rg_a, b_rg_a, w_rg_i,
           b_rg_i, lru_lambda, w_branch_attn, w_branch_lru, w_out, mlp_pre_norm, mlp_post_norm,
           w_mlp_up, w_mlp_down):
    p = dict(attn_pre_norm=attn_pre_norm, attn_post_norm=attn_post_norm, w_in=w_in, conv_w=conv_w,
             conv_b=conv_b, w_rg_a=w_rg_a, b_rg_a=b_rg_a, w_rg_i=w_rg_i, b_rg_i=b_rg_i,
             lru_lambda=lru_lambda, w_branch_attn=w_branch_attn, w_branch_lru=w_branch_lru,
             w_out=w_out, mlp_pre_norm=mlp_pre_norm, mlp_post_norm=mlp_post_norm,
             w_mlp_up=w_mlp_up, w_mlp_down=w_mlp_down)
    h = x
    for l in range(w_in.shape[0]):
        h = _layer(h, p, l)
    return h
```

```python
import functools

import jax
import jax.numpy as jnp
from jax import lax
from jax.experimental import pallas as pl
from jax.experimental.pallas import tpu as pltpu

N_HEADS = 8
HEAD_DIM = 128
ROT_DIM = 32
ROPE_THETA = 500000.0
MOBA_BLOCK = 256
MOBA_TOPK = 3
LRU_BLOCKS = 4
CONV_WIDTH = 4
LRU_C = 8.0
NORM_EPS = 1e-6
MASK_VALUE = -1e30

LANES = 128
SUBLANES = 8
VMEM_LIMIT = 56 * 1024 * 1024


def _rms(xf, gain):
    return xf * lax.rsqrt(jnp.mean(xf * xf, axis=-1, keepdims=True) + NORM_EPS) * gain


def _in_proj_kernel(x_ref, g_ref, w_ref, o_ref, u_ref):
    @pl.when(pl.program_id(1) == 0)
    def _():
        u_ref[...] = _rms(x_ref[...], g_ref[...]).astype(jnp.bfloat16)

    o_ref[...] = jnp.dot(u_ref[...], w_ref[...],
                         preferred_element_type=jnp.float32).astype(o_ref.dtype)


def _in_proj(x2, gain, w_bf16, tm, tn):
    n, d = x2.shape
    cols = w_bf16.shape[1]
    return pl.pallas_call(
        _in_proj_kernel,
        grid=(n // tm, cols // tn),
        in_specs=[
            pl.BlockSpec((tm, d), lambda i, j: (i, 0)),
            pl.BlockSpec((1, d), lambda i, j: (0, 0)),
            pl.BlockSpec((d, tn), lambda i, j: (0, j)),
        ],
        out_specs=pl.BlockSpec((tm, tn), lambda i, j: (i, j)),
        out_shape=jax.ShapeDtypeStruct((n, cols), jnp.bfloat16),
        scratch_shapes=[pltpu.VMEM((tm, d), jnp.bfloat16)],
        compiler_params=pltpu.CompilerParams(
            dimension_semantics=("arbitrary", "arbitrary"), vmem_limit_bytes=VMEM_LIMIT),
        name="in_proj",
    )(x2, gain, w_bf16)


def _rope(xf, c, s):
    lane = lax.broadcasted_iota(jnp.int32, xf.shape, 1)
    half = ROT_DIM // 2
    partner = jnp.where(lane < half, pltpu.roll(xf, LANES - half, axis=1),
                        pltpu.roll(xf, half, axis=1))
    return xf * c + partner * s


def _split_bf16(xf):
    hi = xf.astype(jnp.bfloat16)
    lo = (xf - hi.astype(jnp.float32)).astype(jnp.bfloat16)
    return hi, lo


def _nt_dot(a, b):
    return lax.dot_general(a, b, (((1,), (1,)), ((), ())), preferred_element_type=jnp.float32)


def _moba_kernel(q_ref, k_ref, v_ref, c_ref, s_ref, o_ref,
                 kr_ref, vt_ref, kmh_ref, kml_ref, sel_ref, sc_ref, st_ref, acc_ref,
                 *, n_blocks, grp, qb):
    u = pl.program_id(2)
    blk = MOBA_BLOCK
    nq = qb * blk
    i0 = u * qb

    @pl.when(u == 0)
    def _():
        def body(c, carry):
            rows = pl.ds(pl.multiple_of(c * blk, blk), blk)
            kc = _rope(k_ref[rows, :].astype(jnp.float32), c_ref[rows, :], s_ref[rows, :])
            kr_ref[rows, :] = kc.astype(jnp.bfloat16)
            hi, lo = _split_bf16(jnp.mean(kc, axis=0, keepdims=True))
            kmh_ref[pl.ds(c, 1), :] = hi.astype(jnp.float32)
            kml_ref[pl.ds(c, 1), :] = lo.astype(jnp.float32)
            vt_ref[c] = v_ref[rows, :].astype(jnp.float32).T.astype(jnp.bfloat16)
            return carry

        lax.fori_loop(0, n_blocks, body, 0)

    rows_q = pl.ds(pl.multiple_of(u * nq, nq), nq)
    qf = _rope(q_ref[...].astype(jnp.float32), c_ref[rows_q, :], s_ref[rows_q, :])
    qs = (qf * (HEAD_DIM ** -0.5 * 1.4426950408889634)).astype(jnp.bfloat16)

    q_hi, q_lo = _split_bf16(qf)
    km_hi = kmh_ref[...].astype(jnp.bfloat16)
    km_lo = kml_ref[...].astype(jnp.bfloat16)
    gate = _nt_dot(km_hi, q_hi) + (_nt_dot(km_hi, q_lo) + _nt_dot(km_lo, q_hi))
    blk_id = lax.broadcasted_iota(jnp.int32, gate.shape, 0)
    own = i0 + lax.broadcasted_iota(jnp.int32, gate.shape, 1) // blk
    g = jnp.where(blk_id < own, gate, -jnp.inf)
    sel = jnp.zeros(gate.shape, jnp.float32)
    for r in range(MOBA_TOPK):
        mx = jnp.max(g, axis=0, keepdims=True)
        idx = jnp.min(jnp.where(g == mx, blk_id, n_blocks), axis=0, keepdims=True)
        hit = blk_id == idx
        sel = jnp.where(jnp.logical_and(hit, r < own), 1.0, sel)
        g = jnp.where(hit, -jnp.inf, g)
    for j in range(n_blocks):
        sel_ref[j] = sel[j:j + 1, :]

    def group_scores(j0):
        rows = pl.ds(pl.multiple_of(j0 * blk, grp * blk), grp * blk)
        return _nt_dot(kr_ref[rows, :], qs).reshape(grp, blk, nq)

    def fold8(x, op):
        return op(x.reshape(blk // SUBLANES, SUBLANES, nq), axis=0)

    def as_row(x, r):
        st_ref[r:r + 1, :] = x
        return st_ref[r:r + 1, :]

    def consume(slot, jb, live, m, gmax, l8):
        m_new = jnp.maximum(m, gmax)
        alpha = jnp.exp2(m - m_new)
        l8 = alpha * l8
        pv = None
        for c in range(grp):
            pc = jnp.exp2(sc_ref[slot, c] - jnp.where(live[c], m_new, jnp.inf))
            l8 = l8 + fold8(pc, jnp.sum)
            d = jnp.dot(vt_ref[jb + c], pc.astype(jnp.bfloat16),
                        preferred_element_type=jnp.float32)
            pv = d if pv is None else pv + d
        acc_ref[...] = alpha * acc_ref[...] + pv
        return m_new, l8

    jd = pl.multiple_of((i0 // grp) * grp, grp)
    bj = jd + lax.broadcasted_iota(jnp.int32, (grp, 1, nq), 0)
    qpos = lax.broadcasted_iota(jnp.int32, (grp, 1, nq), 2)
    visible = jnp.where(sel_ref[pl.ds(jd, grp)] > 0.0, blk,
                        jnp.where(bj == i0 + qpos // blk, qpos % blk + 1, 0))
    kpos = lax.broadcasted_iota(jnp.int32, (grp, blk, nq), 1)
    sd = jnp.where(kpos < visible, group_scores(jd), MASK_VALUE)
    sc_ref[0] = sd
    g0 = as_row(jnp.max(jnp.max(sd, axis=0), axis=0, keepdims=True), 0)
    acc_ref[...] = jnp.zeros_like(acc_ref)
    n_past = i0 // grp

    def live_rows(k, jb):
        return [jnp.logical_or(sel_ref[jb + c] > 0.0, k == 0) for c in range(grp)]

    def body(k, carry):
        m, gmax, l8 = carry
        slot = lax.rem(k, 2)
        jb = jnp.where(k == 0, jd, (k - 1) * grp)
        m, l8 = consume(slot, jb, live_rows(k, jb), m, gmax, l8)
        j0 = pl.multiple_of(k * grp, grp)
        sg = group_scores(j0)
        sc_ref[1 - slot] = sg
        bmax = [jnp.where(sel_ref[j0 + c] > 0.0, fold8(sg[c], jnp.max), -jnp.inf)
                for c in range(grp)]
        gnext = jnp.max(functools.reduce(jnp.maximum, bmax), axis=0, keepdims=True)
        return m, as_row(gnext, 1), l8

    init = (g0, g0, jnp.zeros((SUBLANES, nq), jnp.float32))
    m, gmax, l8 = lax.fori_loop(0, n_past, body, init)
    jb = jnp.where(n_past == 0, jd, (n_past - 1) * grp)
    _, l8 = consume(lax.rem(n_past, 2), jb, live_rows(n_past, jb), m, gmax, l8)
    o_ref[...] = (acc_ref[...] / jnp.sum(l8, axis=0, keepdims=True)).T.astype(o_ref.dtype)


def _moba_attention(proj, cos_t, sin_t, batch, seq):
    n_blocks = seq // MOBA_BLOCK
    n = batch * seq
    grp = 4 if n_blocks % 4 == 0 else 2
    qb = 2
    assert n_blocks % grp == 0 and grp % qb == 0
    nq = qb * MOBA_BLOCK
    n_steps = n_blocks // qb
    kern = functools.partial(_moba_kernel, n_blocks=n_blocks, grp=grp, qb=qb)
    return pl.pallas_call(
        kern,
        grid=(batch, N_HEADS, n_steps),
        in_specs=[
            pl.BlockSpec((nq, HEAD_DIM), lambda b, h, u: (b * n_steps + u, h)),
            pl.BlockSpec((seq, HEAD_DIM), lambda b, h, u: (b, N_HEADS + h)),
            pl.BlockSpec((seq, HEAD_DIM), lambda b, h, u: (b, 2 * N_HEADS + h)),
            pl.BlockSpec((seq, LANES), lambda b, h, u: (0, 0)),
            pl.BlockSpec((seq, LANES), lambda b, h, u: (0, 0)),
        ],
        out_specs=pl.BlockSpec((nq, HEAD_DIM), lambda b, h, u: (b * n_steps + u, h)),
        out_shape=jax.ShapeDtypeStruct((n, N_HEADS * HEAD_DIM), jnp.bfloat16),
        scratch_shapes=[pltpu.VMEM((seq, HEAD_DIM), jnp.bfloat16),
                        pltpu.VMEM((n_blocks, HEAD_DIM, MOBA_BLOCK), jnp.bfloat16),
                        pltpu.VMEM((n_blocks, HEAD_DIM), jnp.float32),
                        pltpu.VMEM((n_blocks, HEAD_DIM), jnp.float32),
                        pltpu.VMEM((n_blocks, 1, nq), jnp.float32),
                        pltpu.VMEM((2, grp, MOBA_BLOCK, nq), jnp.float32),
                        pltpu.VMEM((SUBLANES, nq), jnp.float32),
                        pltpu.VMEM((HEAD_DIM, nq), jnp.float32)],
        compiler_params=pltpu.CompilerParams(
            dimension_semantics=("arbitrary", "arbitrary", "arbitrary"),
            vmem_limit_bytes=VMEM_LIMIT),
        name="moba_attn",
    )(proj, proj, proj, cos_t, sin_t)


def _gelu_tanh(x):
    return 0.5 * x * (1.0 + jnp.tanh(0.7978845608028654 * (x + 0.044715 * (x * x * x))))


def _sigmoid(x):
    return 1.0 / (1.0 + jnp.exp(-x))


def _rglru_kernel(x_ref, g_ref, cw_ref, cb_ref, wa_ref, ba_ref, wi_ref, bi_ref, lam_ref,
                  o_ref, xe_ref, h_ref, *, t):
    c = pl.program_id(1)
    pad = SUBLANES

    @pl.when(c == 0)
    def _():
        xe_ref[0:pad, :] = jnp.zeros((pad, xe_ref.shape[1]), jnp.float32)
        h_ref[...] = jnp.zeros_like(h_ref)

    @pl.when(c > 0)
    def _():
        xe_ref[0:pad, :] = xe_ref[t:t + pad, :]

    xe_ref[pad:pad + t, :] = x_ref[...].astype(jnp.float32)

    xc = cb_ref[...]
    for j in range(CONV_WIDTH):
        lo = pad - (CONV_WIDTH - 1) + j
        xc = xc + cw_ref[j:j + 1, :] * xe_ref[lo:lo + t, :]

    width = xc.shape[1]
    bw = width // LRU_BLOCKS
    xb = xc.astype(jnp.bfloat16)
    ra = jnp.concatenate(
        [jnp.dot(xb[:, g * bw:(g + 1) * bw], wa_ref[g], preferred_element_type=jnp.float32)
         for g in range(LRU_BLOCKS)], axis=1) + ba_ref[...]
    ri = jnp.concatenate(
        [jnp.dot(xb[:, g * bw:(g + 1) * bw], wi_ref[g], preferred_element_type=jnp.float32)
         for g in range(LRU_BLOCKS)], axis=1) + bi_ref[...]
    r = _sigmoid(ra)
    gi = _sigmoid(ri)
    z = -lam_ref[...]
    softplus = jnp.maximum(z, 0.0) + jnp.log(1.0 + jnp.exp(-jnp.abs(z)))
    a = jnp.exp((-LRU_C) * r * softplus)
    b = jnp.sqrt(1.0 - a * a) * (gi * xc)

    row = lax.broadcasted_iota(jnp.int32, a.shape, 0)
    d = 1
    while d < t:
        a_sh = pltpu.roll(a, d, axis=0)
        b_sh = pltpu.roll(b, d, axis=0)
        keep = row >= d
        b = jnp.where(keep, a * b_sh + b, b)
        a = jnp.where(keep, a * a_sh, a)
        d *= 2
    h = b + a * h_ref[...]
    h_ref[...] = h[t - 1:t, :]
    o_ref[...] = (h * _gelu_tanh(g_ref[...].astype(jnp.float32))).astype(o_ref.dtype)


def _rglru(proj, conv_w, conv_b, wa, ba, wi, bi, lam, batch, seq, t):
    n = batch * seq
    width = conv_w.shape[1]
    nc = seq // t
    xcol = (3 * N_HEADS * HEAD_DIM) // width
    kern = functools.partial(_rglru_kernel, t=t)
    vec = lambda: pl.BlockSpec((1, width), lambda b, c: (0, 0))
    wspec = lambda: pl.BlockSpec((LRU_BLOCKS, width // LRU_BLOCKS, width // LRU_BLOCKS),
                                 lambda b, c: (0, 0, 0))
    return pl.pallas_call(
        kern,
        grid=(batch, nc),
        in_specs=[
            pl.BlockSpec((t, width), lambda b, c: (b * nc + c, xcol)),
            pl.BlockSpec((t, width), lambda b, c: (b * nc + c, xcol + 1)),
            pl.BlockSpec((CONV_WIDTH, width), lambda b, c: (0, 0)),
            vec(), wspec(), vec(), wspec(), vec(), vec(),
        ],
        out_specs=pl.BlockSpec((t, width), lambda b, c: (b * nc + c, 0)),
        out_shape=jax.ShapeDtypeStruct((n, width), jnp.bfloat16),
        scratch_shapes=[pltpu.VMEM((t + 2 * SUBLANES, width), jnp.float32),
                        pltpu.VMEM((1, width), jnp.float32)],
        compiler_params=pltpu.CompilerParams(
            dimension_semantics=("arbitrary", "arbitrary"), vmem_limit_bytes=VMEM_LIMIT),
        name="rglru",
    )(proj, proj, conv_w, conv_b, wa, ba, wi, bi, lam)


def _merge_kernel(x_ref, at_ref, lr_ref, ga_ref, gl_ref, wba_ref, wbl_ref, wo_ref, gain_ref,
                  o_ref):
    ya = jnp.dot(at_ref[...], wba_ref[...], preferred_element_type=jnp.float32)
    yl = jnp.dot(lr_ref[...], wbl_ref[...], preferred_element_type=jnp.float32)
    merged = (_sigmoid(ga_ref[...].astype(jnp.float32)) * ya
              + _sigmoid(gl_ref[...].astype(jnp.float32)) * yl)
    mix = jnp.dot(merged.astype(jnp.bfloat16), wo_ref[...], preferred_element_type=jnp.float32)
    o_ref[...] = x_ref[...] + _rms(mix, gain_ref[...])


def _merge(x2, attn, lru, proj, wba, wbl, wo, gain, tm):
    n, d = x2.shape
    gcol = (3 * N_HEADS * HEAD_DIM + 2 * d) // d
    row = lambda col: pl.BlockSpec((tm, d), lambda i: (i, col))
    wsp = lambda: pl.BlockSpec((d, d), lambda i: (0, 0))
    return pl.pallas_call(
        _merge_kernel,
        grid=(n // tm,),
        in_specs=[row(0), row(0), row(0), row(gcol), row(gcol + 1), wsp(), wsp(), wsp(),
                  pl.BlockSpec((1, d), lambda i: (0, 0))],
        out_specs=row(0),
        out_shape=jax.ShapeDtypeStruct((n, d), jnp.float32),
        compiler_params=pltpu.CompilerParams(
            dimension_semantics=("arbitrary",), vmem_limit_bytes=VMEM_LIMIT),
        name="merge",
    )(x2, attn, lru, proj, proj, wba, wbl, wo, gain)


def _mlp_kernel(h_ref, gpre_ref, wu_ref, wd_ref, gpost_ref, o_ref, *, fc):
    h = h_ref[...]
    u = _rms(h, gpre_ref[...]).astype(jnp.bfloat16)
    d_ff = wu_ref.shape[1]
    acc = jnp.zeros(h.shape, jnp.float32)
    for f in range(d_ff // fc):
        a = jnp.dot(u, wu_ref[:, f * fc:(f + 1) * fc], preferred_element_type=jnp.float32)
        a = jnp.square(jnp.maximum(a, 0.0)).astype(jnp.bfloat16)
        acc = acc + jnp.dot(a, wd_ref[f * fc:(f + 1) * fc, :], preferred_element_type=jnp.float32)
    o_ref[...] = h + _rms(acc, gpost_ref[...])


def _mlp(h1, gpre, wu, wd, gpost, tm, fc):
    n, d = h1.shape
    d_ff = wu.shape[1]
    kern = functools.partial(_mlp_kernel, fc=fc)
    vec = lambda: pl.BlockSpec((1, d), lambda i: (0, 0))
    return pl.pallas_call(
        kern,
        grid=(n // tm,),
        in_specs=[pl.BlockSpec((tm, d), lambda i: (i, 0)), vec(),
                  pl.BlockSpec((d, d_ff), lambda i: (0, 0)),
                  pl.BlockSpec((d_ff, d), lambda i: (0, 0)), vec()],
        out_specs=pl.BlockSpec((tm, d), lambda i: (i, 0)),
        out_shape=jax.ShapeDtypeStruct((n, d), jnp.float32),
        compiler_params=pltpu.CompilerParams(
            dimension_semantics=("arbitrary",), vmem_limit_bytes=VMEM_LIMIT),
        name="mlp",
    )(h1, gpre, wu, wd, gpost)


def _rope_tables(seq):
    half = ROT_DIM // 2
    inv_freq = ROPE_THETA ** (-jnp.arange(0, ROT_DIM, 2, dtype=jnp.float32) / ROT_DIM)
    ang = jnp.arange(seq, dtype=jnp.float32)[:, None] * inv_freq[None, :]
    cos, sin = jnp.cos(ang), jnp.sin(ang)
    ones = jnp.ones((seq, LANES - ROT_DIM), jnp.float32)
    zeros = jnp.zeros((seq, LANES - ROT_DIM), jnp.float32)
    return (jnp.concatenate([cos, cos, ones], axis=1),
            jnp.concatenate([-sin, sin, zeros], axis=1))


def _layer(h, p, l):
    batch, seq, d = h.shape
    n = batch * seq
    bf = jnp.bfloat16
    row = lambda v: v[l].reshape(1, -1)
    x2 = h.reshape(n, d)
    proj = _in_proj(x2, row(p["attn_pre_norm"]), p["w_in"][l].astype(bf), tm=min(1024, n), tn=1024)
    cos_t, sin_t = _rope_tables(seq)
    attn = _moba_attention(proj, cos_t, sin_t, batch, seq)
    lru = _rglru(proj, p["conv_w"][l], row(p["conv_b"]), p["w_rg_a"][l].astype(bf), row(p["b_rg_a"]),
                 p["w_rg_i"][l].astype(bf), row(p["b_rg_i"]), row(p["lru_lambda"]),
                 batch, seq, t=min(256, seq))
    h1 = _merge(x2, attn, lru, proj, p["w_branch_attn"][l].astype(bf),
                p["w_branch_lru"][l].astype(bf), p["w_out"][l].astype(bf),
                row(p["attn_post_norm"]), tm=min(512, n))
    out = _mlp(h1, row(p["mlp_pre_norm"]), p["w_mlp_up"][l].astype(bf),
               p["w_mlp_down"][l].astype(bf), row(p["mlp_post_norm"]), tm=min(512, n), fc=1024)
    return out.reshape(batch, seq, d)


def kernel(x, attn_pre_norm, attn_post_norm, w_in, conv_w, conv_b, w_rg_a, b_rg_a, w_rg_i,
           b_rg_i, lru_lambda, w_branch_attn, w_branch_lru, w_out, mlp_pre_norm, mlp_post_norm,
           w_mlp_up, w_mlp_down):
    p = dict(attn_pre_norm=attn_pre_norm, attn_post_norm=attn_post_norm, w_in=w_in, conv_w=conv_w,
             conv_b=conv_b, w_rg_a=w_rg_a, b_rg_a=b_rg_a, w_rg_i=w_rg_i, b_rg_i=b_rg_i,
             lru_lambda=lru_lambda, w_branch_attn=w_branch_attn, w_branch_lru=w_branch_lru,
             w_out=w_out, mlp_pre_norm=mlp_pre_norm, mlp_post_norm=mlp_post_norm,
             w_mlp_up=w_mlp_up, w_mlp_down=w_mlp_down)
    h = x
    for l in range(w_in.shape[0]):
        h = _layer(h, p, l)
    return h
```

```python
import functools

import jax
import jax.numpy as jnp
from jax import lax
from jax.experimental import pallas as pl
from jax.experimental.pallas import tpu as pltpu

N_HEADS = 8
HEAD_DIM = 128
ROT_DIM = 32
ROPE_THETA = 500000.0
MOBA_BLOCK = 256
MOBA_TOPK = 3
LRU_BLOCKS = 4
CONV_WIDTH = 4
LRU_C = 8.0
NORM_EPS = 1e-6
MASK_VALUE = -1e30

LANES = 128
SUBLANES = 8
VMEM_LIMIT = 56 * 1024 * 1024


def _rms(xf, gain):
    return xf * lax.rsqrt(jnp.mean(xf * xf, axis=-1, keepdims=True) + NORM_EPS) * gain


def _in_proj_kernel(x_ref, g_ref, w_ref, o_ref, u_ref):
    @pl.when(pl.program_id(1) == 0)
    def _():
        u_ref[...] = _rms(x_ref[...], g_ref[...]).astype(jnp.bfloat16)

    o_ref[...] = jnp.dot(u_ref[...], w_ref[...],
                         preferred_element_type=jnp.float32).astype(o_ref.dtype)


def _in_proj(x2, gain, w_bf16, tm, tn):
    n, d = x2.shape
    cols = w_bf16.shape[1]
    return pl.pallas_call(
        _in_proj_kernel,
        grid=(n // tm, cols // tn),
        in_specs=[
            pl.BlockSpec((tm, d), lambda i, j: (i, 0)),
            pl.BlockSpec((1, d), lambda i, j: (0, 0)),
            pl.BlockSpec((d, tn), lambda i, j: (0, j)),
        ],
        out_specs=pl.BlockSpec((tm, tn), lambda i, j: (i, j)),
        out_shape=jax.ShapeDtypeStruct((n, cols), jnp.bfloat16),
        scratch_shapes=[pltpu.VMEM((tm, d), jnp.bfloat16)],
        compiler_params=pltpu.CompilerParams(
            dimension_semantics=("arbitrary", "arbitrary"), vmem_limit_bytes=VMEM_LIMIT),
        name="in_proj",
    )(x2, gain, w_bf16)


def _rope(xf, c, s):
    lane = lax.broadcasted_iota(jnp.int32, xf.shape, 1)
    half = ROT_DIM // 2
    partner = jnp.where(lane < half, pltpu.roll(xf, LANES - half, axis=1),
                        pltpu.roll(xf, half, axis=1))
    return xf * c + partner * s


def _split_bf16(xf):
    hi = xf.astype(jnp.bfloat16)
    lo = (xf - hi.astype(jnp.float32)).astype(jnp.bfloat16)
    return hi, lo


def _nt_dot(a, b):
    return lax.dot_general(a, b, (((1,), (1,)), ((), ())), preferred_element_type=jnp.float32)


def _moba_kernel(q_ref, k_ref, v_ref, c_ref, s_ref, o_ref,
                 kr_ref, vt_ref, kmh_ref, kml_ref, sel_ref, sc_ref, pb_ref, st_ref, acc_ref,
                 *, n_blocks, grp, qb):
    u = pl.program_id(2)
    blk = MOBA_BLOCK
    nq = qb * blk
    i0 = u * qb

    @pl.when(u == 0)
    def _():
        pb_ref[...] = jnp.zeros_like(pb_ref)

        def body(c, carry):
            rows = pl.ds(pl.multiple_of(c * blk, blk), blk)
            kc = _rope(k_ref[rows, :].astype(jnp.float32), c_ref[rows, :], s_ref[rows, :])
            kr_ref[rows, :] = kc.astype(jnp.bfloat16)
            hi, lo = _split_bf16(jnp.mean(kc, axis=0, keepdims=True))
            kmh_ref[pl.ds(c, 1), :] = hi.astype(jnp.float32)
            kml_ref[pl.ds(c, 1), :] = lo.astype(jnp.float32)
            vt_ref[c] = v_ref[rows, :].astype(jnp.float32).T.astype(jnp.bfloat16)
            return carry

        lax.fori_loop(0, n_blocks, body, 0)

    rows_q = pl.ds(pl.multiple_of(u * nq, nq), nq)
    qf = _rope(q_ref[...].astype(jnp.float32), c_ref[rows_q, :], s_ref[rows_q, :])
    qs = (qf * (HEAD_DIM ** -0.5 * 1.4426950408889634)).astype(jnp.bfloat16)

    q_hi, q_lo = _split_bf16(qf)
    km_hi = kmh_ref[...].astype(jnp.bfloat16)
    km_lo = kml_ref[...].astype(jnp.bfloat16)
    gate = _nt_dot(km_hi, q_hi) + (_nt_dot(km_hi, q_lo) + _nt_dot(km_lo, q_hi))
    blk_id = lax.broadcasted_iota(jnp.int32, gate.shape, 0)
    own = i0 + lax.broadcasted_iota(jnp.int32, gate.shape, 1) // blk
    g = jnp.where(blk_id < own, gate, -jnp.inf)
    sel = jnp.zeros(gate.shape, jnp.float32)
    for r in range(MOBA_TOPK):
        mx = jnp.max(g, axis=0, keepdims=True)
        idx = jnp.min(jnp.where(g == mx, blk_id, n_blocks), axis=0, keepdims=True)
        hit = blk_id == idx
        sel = jnp.where(jnp.logical_and(hit, r < own), 1.0, sel)
        g = jnp.where(hit, -jnp.inf, g)
    for j in range(n_blocks):
        sel_ref[j] = sel[j:j + 1, :]

    def group_scores(j0):
        rows = pl.ds(pl.multiple_of(j0 * blk, grp * blk), grp * blk)
        return _nt_dot(kr_ref[rows, :], qs).reshape(grp, blk, nq)

    def fold8(x, op):
        return op(x.reshape(blk // SUBLANES, SUBLANES, nq), axis=0)

    def as_row(x, r):
        st_ref[r:r + 1, :] = x
        return st_ref[r:r + 1, :]

    def item_base(k):
        return jnp.where(k == 0, jd, (k - 1) * grp)

    def probs(k, slot, m, gmax, l8):
        jb = item_base(k)
        m_new = jnp.maximum(m, gmax)
        alpha = jnp.exp2(m - m_new)
        l8 = alpha * l8
        for c in range(grp):
            live = jnp.logical_or(sel_ref[jb + c] > 0.0, k == 0)
            pc = jnp.exp2(sc_ref[slot, c] - jnp.where(live, m_new, jnp.inf))
            l8 = l8 + fold8(pc, jnp.sum)
            pb_ref[slot, c] = pc.astype(jnp.bfloat16)
        return m_new, alpha, l8

    def accumulate(k, slot, alpha):
        jb = item_base(jnp.maximum(k - 1, 0))
        pv = None
        for c in range(grp):
            d = jnp.dot(vt_ref[jb + c], pb_ref[slot, c], preferred_element_type=jnp.float32)
            pv = d if pv is None else pv + d
        acc_ref[...] = alpha * acc_ref[...] + jnp.where(k > 0, pv, 0.0)

    jd = pl.multiple_of((i0 // grp) * grp, grp)
    bj = jd + lax.broadcasted_iota(jnp.int32, (grp, 1, nq), 0)
    qpos = lax.broadcasted_iota(jnp.int32, (grp, 1, nq), 2)
    visible = jnp.where(sel_ref[pl.ds(jd, grp)] > 0.0, blk,
                        jnp.where(bj == i0 + qpos // blk, qpos % blk + 1, 0))
    kpos = lax.broadcasted_iota(jnp.int32, (grp, blk, nq), 1)
    sd = jnp.where(kpos < visible, group_scores(jd), MASK_VALUE)
    sc_ref[0] = sd
    g0 = as_row(jnp.max(jnp.max(sd, axis=0), axis=0, keepdims=True), 0)
    acc_ref[...] = jnp.zeros_like(acc_ref)
    n_past = i0 // grp

    def trip(k, s, carry):
        m, gmax, alpha, l8 = carry
        accumulate(k, 1 - s, alpha)
        m, alpha, l8 = probs(k, s, m, gmax, l8)
        j0 = pl.multiple_of(k * grp, grp)
        sg = group_scores(j0)
        sc_ref[1 - s] = sg
        bmax = [jnp.where(sel_ref[j0 + c] > 0.0, fold8(sg[c], jnp.max), -jnp.inf)
                for c in range(grp)]
        gnext = jnp.max(functools.reduce(jnp.maximum, bmax), axis=0, keepdims=True)
        return m, as_row(gnext, 1), alpha, l8

    def drain(k, s, carry):
        m, gmax, alpha, l8 = carry
        accumulate(k, 1 - s, alpha)
        _, alpha, l8 = probs(k, s, m, gmax, l8)
        accumulate(k + 1, s, alpha)
        return l8

    init = (g0, g0, jnp.exp2(g0 - g0), jnp.zeros((SUBLANES, nq), jnp.float32))
    carry = lax.fori_loop(0, n_past // 2,
                          lambda t, c: trip(2 * t + 1, 1, trip(2 * t, 0, c)), init)
    l8 = lax.cond(lax.rem(n_past, 2) == 1,
                  lambda c: drain(n_past, 1, trip(n_past - 1, 0, c)),
                  lambda c: drain(n_past, 0, c), carry)
    o_ref[...] = (acc_ref[...] / jnp.sum(l8, axis=0, keepdims=True)).T.astype(o_ref.dtype)


def _moba_attention(proj, cos_t, sin_t, batch, seq):
    n_blocks = seq // MOBA_BLOCK
    n = batch * seq
    grp = 4 if n_blocks % 4 == 0 else 2
    qb = 2
    assert n_blocks % grp == 0 and grp % qb == 0
    nq = qb * MOBA_BLOCK
    n_steps = n_blocks // qb
    kern = functools.partial(_moba_kernel, n_blocks=n_blocks, grp=grp, qb=qb)
    return pl.pallas_call(
        kern,
        grid=(batch, N_HEADS, n_steps),
        in_specs=[
            pl.BlockSpec((nq, HEAD_DIM), lambda b, h, u: (b * n_steps + u, h)),
            pl.BlockSpec((seq, HEAD_DIM), lambda b, h, u: (b, N_HEADS + h)),
            pl.BlockSpec((seq, HEAD_DIM), lambda b, h, u: (b, 2 * N_HEADS + h)),
            pl.BlockSpec((seq, LANES), lambda b, h, u: (0, 0)),
            pl.BlockSpec((seq, LANES), lambda b, h, u: (0, 0)),
        ],
        out_specs=pl.BlockSpec((nq, HEAD_DIM), lambda b, h, u: (b * n_steps + u, h)),
        out_shape=jax.ShapeDtypeStruct((n, N_HEADS * HEAD_DIM), jnp.bfloat16),
        scratch_shapes=[pltpu.VMEM((seq, HEAD_DIM), jnp.bfloat16),
                        pltpu.VMEM((n_blocks, HEAD_DIM, MOBA_BLOCK), jnp.bfloat16),
                        pltpu.VMEM((n_blocks, HEAD_DIM), jnp.float32),
                        pltpu.VMEM((n_blocks, HEAD_DIM), jnp.float32),
                        pltpu.VMEM((n_blocks, 1, nq), jnp.float32),
                        pltpu.VMEM((2, grp, MOBA_BLOCK, nq), jnp.float32),
                        pltpu.VMEM((2, grp, MOBA_BLOCK, nq), jnp.bfloat16),
                        pltpu.VMEM((SUBLANES, nq), jnp.float32),
                        pltpu.VMEM((HEAD_DIM, nq), jnp.float32)],
        compiler_params=pltpu.CompilerParams(
            dimension_semantics=("arbitrary", "arbitrary", "arbitrary"),
            vmem_limit_bytes=VMEM_LIMIT),
        name="moba_attn",
    )(proj, proj, proj, cos_t, sin_t)


def _gelu_tanh(x):
    return 0.5 * x * (1.0 + jnp.tanh(0.7978845608028654 * (x + 0.044715 * (x * x * x))))


def _sigmoid(x):
    return 0.5 * jnp.tanh(0.5 * x) + 0.5


def _rglru_kernel(x_ref, g_ref, cw_ref, cb_ref, wa_ref, ba_ref, wi_ref, bi_ref, lam_ref,
                  o_ref, xt_ref, h_ref, *, t):
    c = pl.program_id(1)
    width = x_ref.shape[1]
    ng = t // SUBLANES

    @pl.when(c == 0)
    def _():
        xt_ref[...] = jnp.zeros_like(xt_ref)
        h_ref[...] = jnp.zeros_like(h_ref)

    x3 = x_ref[...].astype(jnp.float32).reshape(ng, SUBLANES, width)
    tail = xt_ref[...]
    xt_ref[...] = x3[ng - 1]
    sub = lax.broadcasted_iota(jnp.int32, x3.shape, 1)
    xc3 = cb_ref[...][None] + cw_ref[CONV_WIDTH - 1:CONV_WIDTH, :][None] * x3
    for k in range(1, CONV_WIDTH):
        rot = pltpu.roll(x3, k, axis=1)
        rot_prev = jnp.concatenate([pltpu.roll(tail, k, axis=0)[None], rot[:-1]], axis=0)
        j = CONV_WIDTH - 1 - k
        xc3 = xc3 + cw_ref[j:j + 1, :][None] * jnp.where(sub >= k, rot, rot_prev)
    xc = xc3.reshape(t, width)

    bw = width // LRU_BLOCKS
    xb = xc.astype(jnp.bfloat16)
    ra = jnp.concatenate(
        [jnp.dot(xb[:, g * bw:(g + 1) * bw], wa_ref[g], preferred_element_type=jnp.float32)
         for g in range(LRU_BLOCKS)], axis=1) + ba_ref[...]
    ri = jnp.concatenate(
        [jnp.dot(xb[:, g * bw:(g + 1) * bw], wi_ref[g], preferred_element_type=jnp.float32)
         for g in range(LRU_BLOCKS)], axis=1) + bi_ref[...]
    r = _sigmoid(ra)
    gi = _sigmoid(ri)
    z = -lam_ref[...]
    softplus = jnp.maximum(z, 0.0) + jnp.log(1.0 + jnp.exp(-jnp.abs(z)))
    a = jnp.exp((-LRU_C) * r * softplus)
    b = jnp.sqrt(1.0 - a * a) * (gi * xc)

    a3 = a.reshape(ng, SUBLANES, width)
    b3 = b.reshape(ng, SUBLANES, width)
    sub = lax.broadcasted_iota(jnp.int32, a3.shape, 1)
    d = 1
    while d < SUBLANES:
        a_sh = pltpu.roll(a3, d, axis=1)
        b_sh = pltpu.roll(b3, d, axis=1)
        keep = sub >= d
        b3 = jnp.where(keep, a3 * b_sh + b3, b3)
        a3 = jnp.where(keep, a3 * a_sh, a3)
        d *= 2
    gate = _gelu_tanh(g_ref[...].astype(jnp.float32)).reshape(ng, SUBLANES, width)
    outs = []
    for g in range(ng):
        hg = b3[g] + a3[g] * h_ref[...]
        h_ref[...] = hg[SUBLANES - 1:SUBLANES, :]
        outs.append(hg * gate[g])
    o_ref[...] = jnp.concatenate(outs, axis=0).astype(o_ref.dtype)


def _rglru(proj, conv_w, conv_b, wa, ba, wi, bi, lam, batch, seq, t):
    n = batch * seq
    width = conv_w.shape[1]
    nc = seq // t
    xcol = (3 * N_HEADS * HEAD_DIM) // width
    kern = functools.partial(_rglru_kernel, t=t)
    vec = lambda: pl.BlockSpec((1, width), lambda b, c: (0, 0))
    wspec = lambda: pl.BlockSpec((LRU_BLOCKS, width // LRU_BLOCKS, width // LRU_BLOCKS),
                                 lambda b, c: (0, 0, 0))
    return pl.pallas_call(
        kern,
        grid=(batch, nc),
        in_specs=[
            pl.BlockSpec((t, width), lambda b, c: (b * nc + c, xcol)),
            pl.BlockSpec((t, width), lambda b, c: (b * nc + c, xcol + 1)),
            pl.BlockSpec((CONV_WIDTH, width), lambda b, c: (0, 0)),
            vec(), wspec(), vec(), wspec(), vec(), vec(),
        ],
        out_specs=pl.BlockSpec((t, width), lambda b, c: (b * nc + c, 0)),
        out_shape=jax.ShapeDtypeStruct((n, width), jnp.bfloat16),
        scratch_shapes=[pltpu.VMEM((SUBLANES, width), jnp.float32),
                        pltpu.VMEM((1, width), jnp.float32)],
        compiler_params=pltpu.CompilerParams(
            dimension_semantics=("arbitrary", "arbitrary"), vmem_limit_bytes=VMEM_LIMIT),
        name="rglru",
    )(proj, proj, conv_w, conv_b, wa, ba, wi, bi, lam)


def _merge_kernel(x_ref, at_ref, lr_ref, ga_ref, gl_ref, wba_ref, wbl_ref, wo_ref, gain_ref,
                  o_ref):
    ya = jnp.dot(at_ref[...], wba_ref[...], preferred_element_type=jnp.float32)
    yl = jnp.dot(lr_ref[...], wbl_ref[...], preferred_element_type=jnp.float32)
    merged = (_sigmoid(ga_ref[...].astype(jnp.float32)) * ya
              + _sigmoid(gl_ref[...].astype(jnp.float32)) * yl)
    mix = jnp.dot(merged.astype(jnp.bfloat16), wo_ref[...], preferred_element_type=jnp.float32)
    o_ref[...] = x_ref[...] + _rms(mix, gain_ref[...])


def _merge(x2, attn, lru, proj, wba, wbl, wo, gain, tm):
    n, d = x2.shape
    gcol = (3 * N_HEADS * HEAD_DIM + 2 * d) // d
    row = lambda col: pl.BlockSpec((tm, d), lambda i: (i, col))
    wsp = lambda: pl.BlockSpec((d, d), lambda i: (0, 0))
    return pl.pallas_call(
        _merge_kernel,
        grid=(n // tm,),
        in_specs=[row(0), row(0), row(0), row(gcol), row(gcol + 1), wsp(), wsp(), wsp(),
                  pl.BlockSpec((1, d), lambda i: (0, 0))],
        out_specs=row(0),
        out_shape=jax.ShapeDtypeStruct((n, d), jnp.float32),
        compiler_params=pltpu.CompilerParams(
            dimension_semantics=("arbitrary",), vmem_limit_bytes=VMEM_LIMIT),
        name="merge",
    )(x2, attn, lru, proj, proj, wba, wbl, wo, gain)


def _mlp_kernel(h_ref, gpre_ref, wu_ref, wd_ref, gpost_ref, o_ref, *, fc):
    h = h_ref[...]
    u = _rms(h, gpre_ref[...]).astype(jnp.bfloat16)
    d_ff = wu_ref.shape[1]
    acc = jnp.zeros(h.shape, jnp.float32)
    for f in range(d_ff // fc):
        a = jnp.dot(u, wu_ref[:, f * fc:(f + 1) * fc], preferred_element_type=jnp.float32)
        a = jnp.square(jnp.maximum(a, 0.0)).astype(jnp.bfloat16)
        acc = acc + jnp.dot(a, wd_ref[f * fc:(f + 1) * fc, :], preferred_element_type=jnp.float32)
    o_ref[...] = h + _rms(acc, gpost_ref[...])


def _mlp(h1, gpre, wu, wd, gpost, tm, fc):
    n, d = h1.shape
    d_ff = wu.shape[1]
    kern = functools.partial(_mlp_kernel, fc=fc)
    vec = lambda: pl.BlockSpec((1, d), lambda i: (0, 0))
    return pl.pallas_call(
        kern,
        grid=(n // tm,),
        in_specs=[pl.BlockSpec((tm, d), lambda i: (i, 0)), vec(),
                  pl.BlockSpec((d, d_ff), lambda i: (0, 0)),
                  pl.BlockSpec((d_ff, d), lambda i: (0, 0)), vec()],
        out_specs=pl.BlockSpec((tm, d), lambda i: (i, 0)),
        out_shape=jax.ShapeDtypeStruct((n, d), jnp.float32),
        compiler_params=pltpu.CompilerParams(
            dimension_semantics=("arbitrary",), vmem_limit_bytes=VMEM_LIMIT),
        name="mlp",
    )(h1, gpre, wu, wd, gpost)


def _rope_tables(seq):
    half = ROT_DIM // 2
    inv_freq = ROPE_THETA ** (-jnp.arange(0, ROT_DIM, 2, dtype=jnp.float32) / ROT_DIM)
    ang = jnp.arange(seq, dtype=jnp.float32)[:, None] * inv_freq[None, :]
    cos, sin = jnp.cos(ang), jnp.sin(ang)
    ones = jnp.ones((seq, LANES - ROT_DIM), jnp.float32)
    zeros = jnp.zeros((seq, LANES - ROT_DIM), jnp.float32)
    return (jnp.concatenate([cos, cos, ones], axis=1),
            jnp.concatenate([-sin, sin, zeros], axis=1))


def _layer(h, p, l):
    batch, seq, d = h.shape
    n = batch * seq
    bf = jnp.bfloat16
    row = lambda v: v[l].reshape(1, -1)
    x2 = h.reshape(n, d)
    proj = _in_proj(x2, row(p["attn_pre_norm"]), p["w_in"][l].astype(bf), tm=min(1024, n), tn=1024)
    cos_t, sin_t = _rope_tables(seq)
    attn = _moba_attention(proj, cos_t, sin_t, batch, seq)
    lru = _rglru(proj, p["conv_w"][l], row(p["conv_b"]), p["w_rg_a"][l].astype(bf), row(p["b_rg_a"]),
                 p["w_rg_i"][l].astype(bf), row(p["b_rg_i"]), row(p["lru_lambda"]),
                 batch, seq, t=min(256, seq))
    h1 = _merge(x2, attn, lru, proj, p["w_branch_attn"][l].astype(bf),
                p["w_branch_lru"][l].astype(bf), p["w_out"][l].astype(bf),
                row(p["attn_post_norm"]), tm=min(512, n))
    out = _mlp(h1, row(p["mlp_pre_norm"]), p["w_mlp_up"][l].astype(bf),
               p["w_mlp_down"][l].astype(bf), row(p["mlp_post_norm"]), tm=min(512, n), fc=1024)
    return out.reshape(batch, seq, d)


def kernel(x, attn_pre_norm, attn_post_norm, w_in, conv_w, conv_b, w_rg_a, b_rg_a, w_rg_i,
           b_rg_i, lru_lambda, w_branch_attn, w_branch_lru, w_out, mlp_pre_norm, mlp_post_norm,
           w_mlp_up, w_mlp_down):
    p = dict(attn_pre_norm=attn_pre_norm, attn_post_norm=attn_post_norm, w_in=w_in, conv_w=conv_w,
             conv_b=conv_b, w_rg_a=w_rg_a, b_rg_a=b_rg_a, w_rg_i=w_rg_i, b_rg_i=b_rg_i,
             lru_lambda=lru_lambda, w_branch_attn=w_branch_attn, w_branch_lru=w_branch_lru,
             w_out=w_out, mlp_pre_norm=mlp_pre_norm, mlp_post_norm=mlp_post_norm,
             w_mlp_up=w_mlp_up, w_mlp_down=w_mlp_down)
    h = x
    for l in range(w_in.shape[0]):
        h = _layer(h, p, l)
    return h
```

```python
import functools

import jax
import jax.numpy as jnp
from jax import lax
from jax.experimental import pallas as pl
from jax.experimental.pallas import tpu as pltpu

N_HEADS = 8
HEAD_DIM = 128
ROT_DIM = 32
ROPE_THETA = 500000.0
MOBA_BLOCK = 256
MOBA_TOPK = 3
LRU_BLOCKS = 4
CONV_WIDTH = 4
LRU_C = 8.0
NORM_EPS = 1e-6
MASK_VALUE = -1e30

LANES = 128
SUBLANES = 8
VMEM_LIMIT = 56 * 1024 * 1024


def _rms(xf, gain):
    return xf * lax.rsqrt(jnp.mean(xf * xf, axis=-1, keepdims=True) + NORM_EPS) * gain


def _in_proj_kernel(x_ref, g_ref, w_ref, o_ref, u_ref):
    @pl.when(pl.program_id(1) == 0)
    def _():
        u_ref[...] = _rms(x_ref[...], g_ref[...]).astype(jnp.bfloat16)

    o_ref[...] = jnp.dot(u_ref[...], w_ref[...],
                         preferred_element_type=jnp.float32).astype(o_ref.dtype)


def _in_proj(x2, gain, w_bf16, tm, tn):
    n, d = x2.shape
    cols = w_bf16.shape[1]
    return pl.pallas_call(
        _in_proj_kernel,
        grid=(n // tm, cols // tn),
        in_specs=[
            pl.BlockSpec((tm, d), lambda i, j: (i, 0)),
            pl.BlockSpec((1, d), lambda i, j: (0, 0)),
            pl.BlockSpec((d, tn), lambda i, j: (0, j)),
        ],
        out_specs=pl.BlockSpec((tm, tn), lambda i, j: (i, j)),
        out_shape=jax.ShapeDtypeStruct((n, cols), jnp.bfloat16),
        scratch_shapes=[pltpu.VMEM((tm, d), jnp.bfloat16)],
        compiler_params=pltpu.CompilerParams(
            dimension_semantics=("arbitrary", "arbitrary"), vmem_limit_bytes=VMEM_LIMIT),
        name="in_proj",
    )(x2, gain, w_bf16)


def _rope(xf, c, s):
    lane = lax.broadcasted_iota(jnp.int32, xf.shape, 1)
    half = ROT_DIM // 2
    partner = jnp.where(lane < half, pltpu.roll(xf, LANES - half, axis=1),
                        pltpu.roll(xf, half, axis=1))
    return xf * c + partner * s


def _split_bf16(xf):
    hi = xf.astype(jnp.bfloat16)
    lo = (xf - hi.astype(jnp.float32)).astype(jnp.bfloat16)
    return hi, lo


def _nt_dot(a, b):
    return lax.dot_general(a, b, (((1,), (1,)), ((), ())), preferred_element_type=jnp.float32)


def _moba_kernel(q_ref, k_ref, v_ref, c_ref, s_ref, o_ref,
                 kr_ref, vt_ref, kmh_ref, kml_ref, sel_ref, sc_ref, pb_ref, st_ref, acc_ref,
                 *, n_blocks, grp, qb, nh):
    u = pl.program_id(2)
    blk = MOBA_BLOCK
    nq = qb * blk
    i0 = u * qb
    jd = pl.multiple_of((i0 // grp) * grp, grp)
    n_past = i0 // grp

    def fold8(x, op):
        return op(x.reshape(blk // SUBLANES, SUBLANES, nq), axis=0)

    def item_base(k):
        return jnp.where(k == 0, jd, (k - 1) * grp)

    class Head:
        def __init__(self, hh):
            self.cols = slice(hh * HEAD_DIM, (hh + 1) * HEAD_DIM)
            self.kr, self.vt, self.kmh, self.kml = (
                r.at[hh] for r in (kr_ref, vt_ref, kmh_ref, kml_ref))
            self.sel, self.sc, self.pb, self.st, self.acc = (
                r.at[hh] for r in (sel_ref, sc_ref, pb_ref, st_ref, acc_ref))

        def setup_block(self, c):
            rows = pl.ds(pl.multiple_of(c * blk, blk), blk)
            kc = _rope(k_ref[rows, self.cols].astype(jnp.float32), c_ref[rows, :], s_ref[rows, :])
            self.kr[rows, :] = kc.astype(jnp.bfloat16)
            hi, lo = _split_bf16(jnp.mean(kc, axis=0, keepdims=True))
            self.kmh[pl.ds(c, 1), :] = hi.astype(jnp.float32)
            self.kml[pl.ds(c, 1), :] = lo.astype(jnp.float32)
            self.vt[c] = v_ref[rows, self.cols].astype(jnp.float32).T.astype(jnp.bfloat16)

        def as_row(self, x, r):
            self.st[r:r + 1, :] = x
            return self.st[r:r + 1, :]

        def group_scores(self, j0):
            rows = pl.ds(pl.multiple_of(j0 * blk, grp * blk), grp * blk)
            return _nt_dot(self.kr[rows, :], self.qs).reshape(grp, blk, nq)

        def prologue(self):
            rows_q = pl.ds(pl.multiple_of(u * nq, nq), nq)
            qf = _rope(q_ref[:, self.cols].astype(jnp.float32), c_ref[rows_q, :], s_ref[rows_q, :])
            self.qs = (qf * (HEAD_DIM ** -0.5 * 1.4426950408889634)).astype(jnp.bfloat16)

            q_hi, q_lo = _split_bf16(qf)
            km_hi = self.kmh[...].astype(jnp.bfloat16)
            km_lo = self.kml[...].astype(jnp.bfloat16)
            gate = _nt_dot(km_hi, q_hi) + (_nt_dot(km_hi, q_lo) + _nt_dot(km_lo, q_hi))
            blk_id = lax.broadcasted_iota(jnp.int32, gate.shape, 0)
            own = i0 + lax.broadcasted_iota(jnp.int32, gate.shape, 1) // blk
            g = jnp.where(blk_id < own, gate, -jnp.inf)
            sel = jnp.zeros(gate.shape, jnp.float32)
            for r in range(MOBA_TOPK):
                mx = jnp.max(g, axis=0, keepdims=True)
                idx = jnp.min(jnp.where(g == mx, blk_id, n_blocks), axis=0, keepdims=True)
                hit = blk_id == idx
                sel = jnp.where(jnp.logical_and(hit, r < own), 1.0, sel)
                g = jnp.where(hit, -jnp.inf, g)
            for j in range(n_blocks):
                self.sel[j] = sel[j:j + 1, :]

            bj = jd + lax.broadcasted_iota(jnp.int32, (grp, 1, nq), 0)
            qpos = lax.broadcasted_iota(jnp.int32, (grp, 1, nq), 2)
            visible = jnp.where(self.sel[pl.ds(jd, grp)] > 0.0, blk,
                                jnp.where(bj == i0 + qpos // blk, qpos % blk + 1, 0))
            kpos = lax.broadcasted_iota(jnp.int32, (grp, blk, nq), 1)
            sd = jnp.where(kpos < visible, self.group_scores(jd), MASK_VALUE)
            self.sc[0] = sd
            g0 = self.as_row(jnp.max(jnp.max(sd, axis=0), axis=0, keepdims=True), 0)
            self.acc[...] = jnp.zeros_like(self.acc)
            return g0, g0, jnp.exp2(g0 - g0), jnp.zeros((SUBLANES, nq), jnp.float32)

        def probs(self, k, slot, m, gmax, l8):
            jb = item_base(k)
            m_new = jnp.maximum(m, gmax)
            alpha = jnp.exp2(m - m_new)
            l8 = alpha * l8
            for c in range(grp):
                live = jnp.logical_or(self.sel[jb + c] > 0.0, k == 0)
                pc = jnp.exp2(self.sc[slot, c] - jnp.where(live, m_new, jnp.inf))
                l8 = l8 + fold8(pc, jnp.sum)
                self.pb[slot, c] = pc.astype(jnp.bfloat16)
            return m_new, alpha, l8

        def accumulate(self, k, slot, alpha):
            jb = item_base(jnp.maximum(k - 1, 0))
            pv = None
            for c in range(grp):
                d = jnp.dot(self.vt[jb + c], self.pb[slot, c], preferred_element_type=jnp.float32)
                pv = d if pv is None else pv + d
            self.acc[...] = alpha * self.acc[...] + jnp.where(k > 0, pv, 0.0)

        def trip(self, k, s, carry):
            m, gmax, alpha, l8 = carry
            self.accumulate(k, 1 - s, alpha)
            m, alpha, l8 = self.probs(k, s, m, gmax, l8)
            j0 = pl.multiple_of(k * grp, grp)
            sg = self.group_scores(j0)
            self.sc[1 - s] = sg
            bmax = [jnp.where(self.sel[j0 + c] > 0.0, fold8(sg[c], jnp.max), -jnp.inf)
                    for c in range(grp)]
            gnext = jnp.max(functools.reduce(jnp.maximum, bmax), axis=0, keepdims=True)
            return m, self.as_row(gnext, 1), alpha, l8

        def drain(self, k, s, carry):
            m, gmax, alpha, l8 = carry
            self.accumulate(k, 1 - s, alpha)
            _, alpha, l8 = self.probs(k, s, m, gmax, l8)
            self.accumulate(k + 1, s, alpha)
            return l8

        def finish(self, l8):
            out = self.acc[...] / jnp.sum(l8, axis=0, keepdims=True)
            o_ref[:, self.cols] = out.T.astype(o_ref.dtype)

    heads = [Head(hh) for hh in range(nh)]

    @pl.when(u == 0)
    def _():
        pb_ref[...] = jnp.zeros_like(pb_ref)

        def body(c, carry):
            for h in heads:
                h.setup_block(c)
            return carry

        lax.fori_loop(0, n_blocks, body, 0)

    def all_heads(fn, carries):
        return tuple(fn(h, c) for h, c in zip(heads, carries))

    carries = tuple(h.prologue() for h in heads)
    carries = lax.fori_loop(
        0, n_past // 2,
        lambda t, cs: all_heads(lambda h, c: h.trip(2 * t + 1, 1, c),
                                all_heads(lambda h, c: h.trip(2 * t, 0, c), cs)),
        carries)
    sums = lax.cond(
        lax.rem(n_past, 2) == 1,
        lambda cs: all_heads(lambda h, c: h.drain(n_past, 1, c),
                             all_heads(lambda h, c: h.trip(n_past - 1, 0, c), cs)),
        lambda cs: all_heads(lambda h, c: h.drain(n_past, 0, c), cs),
        carries)
    for h, l8 in zip(heads, sums):
        h.finish(l8)


def _moba_attention(proj, cos_t, sin_t, batch, seq):
    n_blocks = seq // MOBA_BLOCK
    n = batch * seq
    grp = 4 if n_blocks % 4 == 0 else 2
    qb = 2
    nh = 2
    assert n_blocks % grp == 0 and grp % qb == 0 and N_HEADS % nh == 0
    nq = qb * MOBA_BLOCK
    n_steps = n_blocks // qb
    hw = nh * HEAD_DIM
    kcol = N_HEADS // nh
    kern = functools.partial(_moba_kernel, n_blocks=n_blocks, grp=grp, qb=qb, nh=nh)
    table = lambda: pl.BlockSpec((seq, LANES), lambda b, h, u: (0, 0),
                                 pipeline_mode=pl.Buffered(1))
    return pl.pallas_call(
        kern,
        grid=(batch, N_HEADS // nh, n_steps),
        in_specs=[
            pl.BlockSpec((nq, hw), lambda b, h, u: (b * n_steps + u, h)),
            pl.BlockSpec((seq, hw), lambda b, h, u: (b, kcol + h)),
            pl.BlockSpec((seq, hw), lambda b, h, u: (b, 2 * kcol + h)),
            table(), table(),
        ],
        out_specs=pl.BlockSpec((nq, hw), lambda b, h, u: (b * n_steps + u, h)),
        out_shape=jax.ShapeDtypeStruct((n, N_HEADS * HEAD_DIM), jnp.bfloat16),
        scratch_shapes=[pltpu.VMEM((nh, seq, HEAD_DIM), jnp.bfloat16),
                        pltpu.VMEM((nh, n_blocks, HEAD_DIM, MOBA_BLOCK), jnp.bfloat16),
                        pltpu.VMEM((nh, n_blocks, HEAD_DIM), jnp.float32),
                        pltpu.VMEM((nh, n_blocks, HEAD_DIM), jnp.float32),
                        pltpu.VMEM((nh, n_blocks, 1, nq), jnp.float32),
                        pltpu.VMEM((nh, 2, grp, MOBA_BLOCK, nq), jnp.float32),
                        pltpu.VMEM((nh, 2, grp, MOBA_BLOCK, nq), jnp.bfloat16),
                        pltpu.VMEM((nh, SUBLANES, nq), jnp.float32),
                        pltpu.VMEM((nh, HEAD_DIM, nq), jnp.float32)],
        compiler_params=pltpu.CompilerParams(
            dimension_semantics=("arbitrary", "arbitrary", "arbitrary"),
            vmem_limit_bytes=VMEM_LIMIT),
        name="moba_attn",
    )(proj, proj, proj, cos_t, sin_t)


def _gelu_tanh(x):
    return 0.5 * x * (1.0 + jnp.tanh(0.7978845608028654 * (x + 0.044715 * (x * x * x))))


def _sigmoid(x):
    return 0.5 * jnp.tanh(0.5 * x) + 0.5


def _rglru_kernel(x_ref, g_ref, cw_ref, cb_ref, wa_ref, ba_ref, wi_ref, bi_ref, lam_ref,
                  o_ref, xt_ref, h_ref, *, t):
    c = pl.program_id(1)
    width = x_ref.shape[1]
    ng = t // SUBLANES

    @pl.when(c == 0)
    def _():
        xt_ref[...] = jnp.zeros_like(xt_ref)
        h_ref[...] = jnp.zeros_like(h_ref)

    x3 = x_ref[...].astype(jnp.float32).reshape(ng, SUBLANES, width)
    tail = xt_ref[...]
    xt_ref[...] = x3[ng - 1]
    sub = lax.broadcasted_iota(jnp.int32, x3.shape, 1)
    xc3 = cb_ref[...][None] + cw_ref[CONV_WIDTH - 1:CONV_WIDTH, :][None] * x3
    for k in range(1, CONV_WIDTH):
        rot = pltpu.roll(x3, k, axis=1)
        rot_prev = jnp.concatenate([pltpu.roll(tail, k, axis=0)[None], rot[:-1]], axis=0)
        j = CONV_WIDTH - 1 - k
        xc3 = xc3 + cw_ref[j:j + 1, :][None] * jnp.where(sub >= k, rot, rot_prev)
    xc = xc3.reshape(t, width)

    bw = width // LRU_BLOCKS
    xb = xc.astype(jnp.bfloat16)
    ra = jnp.concatenate(
        [jnp.dot(xb[:, g * bw:(g + 1) * bw], wa_ref[g], preferred_element_type=jnp.float32)
         for g in range(LRU_BLOCKS)], axis=1) + ba_ref[...]
    ri = jnp.concatenate(
        [jnp.dot(xb[:, g * bw:(g + 1) * bw], wi_ref[g], preferred_element_type=jnp.float32)
         for g in range(LRU_BLOCKS)], axis=1) + bi_ref[...]
    r = _sigmoid(ra)
    gi = _sigmoid(ri)
    z = -lam_ref[...]
    softplus = jnp.maximum(z, 0.0) + jnp.log(1.0 + jnp.exp(-jnp.abs(z)))
    a = jnp.exp((-LRU_C) * r * softplus)
    b = jnp.sqrt(1.0 - a * a) * (gi * xc)

    a3 = a.reshape(ng, SUBLANES, width)
    b3 = b.reshape(ng, SUBLANES, width)
    sub = lax.broadcasted_iota(jnp.int32, a3.shape, 1)
    d = 1
    while d < SUBLANES:
        a_sh = pltpu.roll(a3, d, axis=1)
        b_sh = pltpu.roll(b3, d, axis=1)
        keep = sub >= d
        b3 = jnp.where(keep, a3 * b_sh + b3, b3)
        a3 = jnp.where(keep, a3 * a_sh, a3)
        d *= 2
    gate = _gelu_tanh(g_ref[...].astype(jnp.float32)).reshape(ng, SUBLANES, width)
    outs = []
    for g in range(ng):
        hg = b3[g] + a3[g] * h_ref[...]
        h_ref[...] = hg[SUBLANES - 1:SUBLANES, :]
        outs.append(hg * gate[g])
    o_ref[...] = jnp.concatenate(outs, axis=0).astype(o_ref.dtype)


def _rglru(proj, conv_w, conv_b, wa, ba, wi, bi, lam, batch, seq, t):
    n = batch * seq
    width = conv_w.shape[1]
    nc = seq // t
    xcol = (3 * N_HEADS * HEAD_DIM) // width
    kern = functools.partial(_rglru_kernel, t=t)
    vec = lambda: pl.BlockSpec((1, width), lambda b, c: (0, 0))
    wspec = lambda: pl.BlockSpec((LRU_BLOCKS, width // LRU_BLOCKS, width // LRU_BLOCKS),
                                 lambda b, c: (0, 0, 0))
    return pl.pallas_call(
        kern,
        grid=(batch, nc),
        in_specs=[
            pl.BlockSpec((t, width), lambda b, c: (b * nc + c, xcol)),
            pl.BlockSpec((t, width), lambda b, c: (b * nc + c, xcol + 1)),
            pl.BlockSpec((CONV_WIDTH, width), lambda b, c: (0, 0)),
            vec(), wspec(), vec(), wspec(), vec(), vec(),
        ],
        out_specs=pl.BlockSpec((t, width), lambda b, c: (b * nc + c, 0)),
        out_shape=jax.ShapeDtypeStruct((n, width), jnp.bfloat16),
        scratch_shapes=[pltpu.VMEM((SUBLANES, width), jnp.float32),
                        pltpu.VMEM((1, width), jnp.float32)],
        compiler_params=pltpu.CompilerParams(
            dimension_semantics=("arbitrary", "arbitrary"), vmem_limit_bytes=VMEM_LIMIT),
        name="rglru",
    )(proj, proj, conv_w, conv_b, wa, ba, wi, bi, lam)


def _merge_kernel(x_ref, at_ref, lr_ref, ga_ref, gl_ref, wba_ref, wbl_ref, wo_ref, gain_ref,
                  o_ref):
    ya = jnp.dot(at_ref[...], wba_ref[...], preferred_element_type=jnp.float32)
    yl = jnp.dot(lr_ref[...], wbl_ref[...], preferred_element_type=jnp.float32)
    merged = (_sigmoid(ga_ref[...].astype(jnp.float32)) * ya
              + _sigmoid(gl_ref[...].astype(jnp.float32)) * yl)
    mix = jnp.dot(merged.astype(jnp.bfloat16), wo_ref[...], preferred_element_type=jnp.float32)
    o_ref[...] = x_ref[...] + _rms(mix, gain_ref[...])


def _merge(x2, attn, lru, proj, wba, wbl, wo, gain, tm):
    n, d = x2.shape
    gcol = (3 * N_HEADS * HEAD_DIM + 2 * d) // d
    row = lambda col: pl.BlockSpec((tm, d), lambda i: (i, col))
    wsp = lambda: pl.BlockSpec((d, d), lambda i: (0, 0))
    return pl.pallas_call(
        _merge_kernel,
        grid=(n // tm,),
        in_specs=[row(0), row(0), row(0), row(gcol), row(gcol + 1), wsp(), wsp(), wsp(),
                  pl.BlockSpec((1, d), lambda i: (0, 0))],
        out_specs=row(0),
        out_shape=jax.ShapeDtypeStruct((n, d), jnp.float32),
        compiler_params=pltpu.CompilerParams(
            dimension_semantics=("arbitrary",), vmem_limit_bytes=VMEM_LIMIT),
        name="merge",
    )(x2, attn, lru, proj, proj, wba, wbl, wo, gain)


def _mlp_kernel(h_ref, gpre_ref, wu_ref, wd_ref, gpost_ref, o_ref, *, fc):
    h = h_ref[...]
    u = _rms(h, gpre_ref[...]).astype(jnp.bfloat16)
    d_ff = wu_ref.shape[1]
    acc = jnp.zeros(h.shape, jnp.float32)
    for f in range(d_ff // fc):
        a = jnp.dot(u, wu_ref[:, f * fc:(f + 1) * fc], preferred_element_type=jnp.float32)
        a = jnp.square(jnp.maximum(a, 0.0)).astype(jnp.bfloat16)
        acc = acc + jnp.dot(a, wd_ref[f * fc:(f + 1) * fc, :], preferred_element_type=jnp.float32)
    o_ref[...] = h + _rms(acc, gpost_ref[...])


def _mlp(h1, gpre, wu, wd, gpost, tm, fc):
    n, d = h1.shape
    d_ff = wu.shape[1]
    kern = functools.partial(_mlp_kernel, fc=fc)
    vec = lambda: pl.BlockSpec((1, d), lambda i: (0, 0))
    return pl.pallas_call(
        kern,
        grid=(n // tm,),
        in_specs=[pl.BlockSpec((tm, d), lambda i: (i, 0)), vec(),
                  pl.BlockSpec((d, d_ff), lambda i: (0, 0)),
                  pl.BlockSpec((d_ff, d), lambda i: (0, 0)), vec()],
        out_specs=pl.BlockSpec((tm, d), lambda i: (i, 0)),
        out_shape=jax.ShapeDtypeStruct((n, d), jnp.float32),
        compiler_params=pltpu.CompilerParams(
            dimension_semantics=("arbitrary",), vmem_limit_bytes=VMEM_LIMIT),
        name="mlp",
    )(h1, gpre, wu, wd, gpost)


def _rope_tables(seq):
    half = ROT_DIM // 2
    inv_freq = ROPE_THETA ** (-jnp.arange(0, ROT_DIM, 2, dtype=jnp.float32) / ROT_DIM)
    ang = jnp.arange(seq, dtype=jnp.float32)[:, None] * inv_freq[None, :]
    cos, sin = jnp.cos(ang), jnp.sin(ang)
    ones = jnp.ones((seq, LANES - ROT_DIM), jnp.float32)
    zeros = jnp.zeros((seq, LANES - ROT_DIM), jnp.float32)
    return (jnp.concatenate([cos, cos, ones], axis=1),
            jnp.concatenate([-sin, sin, zeros], axis=1))


def _layer(h, p, l):
    batch, seq, d = h.shape
    n = batch * seq
    bf = jnp.bfloat16
    row = lambda v: v[l].reshape(1, -1)
    x2 = h.reshape(n, d)
    proj = _in_proj(x2, row(p["attn_pre_norm"]), p["w_in"][l].astype(bf),
                    tm=2048 if n % 2048 == 0 else min(1024, n), tn=1024)
    cos_t, sin_t = _rope_tables(seq)
    attn = _moba_attention(proj, cos_t, sin_t, batch, seq)
    lru = _rglru(proj, p["conv_w"][l], row(p["conv_b"]), p["w_rg_a"][l].astype(bf), row(p["b_rg_a"]),
                 p["w_rg_i"][l].astype(bf), row(p["b_rg_i"]), row(p["lru_lambda"]),
                 batch, seq, t=min(256, seq))
    h1 = _merge(x2, attn, lru, proj, p["w_branch_attn"][l].astype(bf),
                p["w_branch_lru"][l].astype(bf), p["w_out"][l].astype(bf),
                row(p["attn_post_norm"]), tm=min(512, n))
    out = _mlp(h1, row(p["mlp_pre_norm"]), p["w_mlp_up"][l].astype(bf),
               p["w_mlp_down"][l].astype(bf), row(p["mlp_post_norm"]), tm=min(512, n), fc=1024)
    return out.reshape(batch, seq, d)


def kernel(x, attn_pre_norm, attn_post_norm, w_in, conv_w, conv_b, w_rg_a, b_rg_a, w_rg_i,
           b_rg_i, lru_lambda, w_branch_attn, w_branch_lru, w_out, mlp_pre_norm, mlp_post_norm,
           w_mlp_up, w_mlp_down):
    p = dict(attn_pre_norm=attn_pre_norm, attn_post_norm=attn_post_norm, w_in=w_in, conv_w=conv_w,
             conv_b=conv_b, w_rg_a=w_rg_a, b_rg_a=b_rg_a, w_rg_i=w_rg_i, b_rg_i=b_rg_i,
             lru_lambda=lru_lambda, w_branch_attn=w_branch_attn, w_branch_lru=w_branch_lru,
             w_out=w_out, mlp_pre_norm=mlp_pre_norm, mlp_post_norm=mlp_post_norm,
             w_mlp_up=w_mlp_up, w_mlp_down=w_mlp_down)
    h = x
    for l in range(w_in.shape[0]):
        h = _layer(h, p, l)
    return h
```
